```python
import math
import jax, jax.numpy as jnp
from jax import lax
import numpy as np

D_MODEL = 1024
BATCH = 2
SEQ = 16384
DEPTH = 2

N_MIXERS = 2
N_HEADS = 16
HEAD_DIM = D_MODEL // N_HEADS
MOBA_BLOCK = 256
MOBA_TOPK = 3
Q_CHUNK = 64
CONV_WIDTH = 31
D_FF = 2816
N_SUB = 3
N_ATTN = (DEPTH + 1) // 2
N_CONV = DEPTH // 2
EPS = 1e-6

kernel_name = "moba_conformer_conv_hybrid_adaln"


def rms_norm(x, g):
    xf = x.astype(jnp.float32)
    y = xf * lax.rsqrt(jnp.mean(xf * xf, axis=-1, keepdims=True) + EPS)
    return (y * g.astype(jnp.float32)).astype(x.dtype)


def layer_norm(x, g, b):
    xf = x.astype(jnp.float32)
    mu = jnp.mean(xf, axis=-1, keepdims=True)
    var = jnp.mean(jnp.square(xf - mu), axis=-1, keepdims=True)
    y = (xf - mu) * lax.rsqrt(var + EPS)
    return (y * g.astype(jnp.float32) + b.astype(jnp.float32)).astype(x.dtype)


def swiglu_ffn(h, w_in, w_out):
    u = h @ w_in
    a, b = jnp.split(u, 2, axis=-1)
    return (jax.nn.silu(a) * b) @ w_out


def alibi_slopes(n_heads):
    return jnp.asarray([2.0 ** (-8.0 * (h + 1) / n_heads) for h in range(n_heads)], dtype=jnp.float32)


def moba_attention(h, w_qkv, g_q, g_k, w_o):
    B, S, _ = h.shape
    H, Dh, BLK, C = N_HEADS, HEAD_DIM, MOBA_BLOCK, Q_CHUNK
    qkv = (h @ w_qkv).reshape(B, S, 3, H, Dh)
    q = rms_norm(qkv[:, :, 0], g_q).transpose(0, 2, 1, 3)
    k = rms_norm(qkv[:, :, 1], g_k).transpose(0, 2, 1, 3)
    v = qkv[:, :, 2].transpose(0, 2, 1, 3)
    s_pad = ((S + BLK - 1) // BLK) * BLK
    pad = [(0, 0), (0, 0), (0, s_pad - S), (0, 0)]
    q, k, v = jnp.pad(q, pad), jnp.pad(k, pad), jnp.pad(v, pad)
    nb = s_pad // BLK
    k_sel = min(MOBA_TOPK, nb)
    kb = k.reshape(B, H, nb, BLK, Dh)
    vb = v.reshape(B, H, nb, BLK, Dh)
    kbar = jnp.mean(kb.astype(jnp.float32), axis=3)
    slopes = alibi_slopes(H)
    scale = 1.0 / math.sqrt(Dh)
    b_idx = jnp.arange(B)[:, None, None, None]
    h_idx = jnp.arange(H)[None, :, None, None]
    neg = jnp.float32(-jnp.inf)

    def chunk_fn(ci):
        t0 = ci * C
        own = t0 // BLK
        qc = lax.dynamic_slice_in_dim(q, t0, C, axis=2)
        t = t0 + jnp.arange(C)
        gate = jnp.einsum('bhcd,bhnd->bhcn', qc.astype(jnp.float32), kbar)
        past = jnp.arange(nb) < own
        gate = jnp.where(past[None, None, None, :], gate, neg)
        _, idx = lax.top_k(gate, k_sel)
        valid = idx < own
        ks = kb[b_idx, h_idx, idx]
        vs = vb[b_idx, h_idx, idx]
        pos_sel = idx[..., None] * BLK + jnp.arange(BLK)
        dist_sel = (t[None, None, :, None, None] - pos_sel).astype(jnp.float32)
        s_sel = jnp.einsum('bhcd,bhckld->bhckl', qc, ks).astype(jnp.float32) * scale
        s_sel = s_sel - slopes[None, :, None, None, None] * jnp.abs(dist_sel)
        s_sel = jnp.where(valid[..., None], s_sel, neg)
        ko = lax.dynamic_index_in_dim(kb, own, axis=2, keepdims=False)
        vo = lax.dynamic_index_in_dim(vb, own, axis=2, keepdims=False)
        pos_own = own * BLK + jnp.arange(BLK)
        dist_own = (t[:, None] - pos_own[None, :]).astype(jnp.float32)
        s_own = jnp.einsum('bhcd,bhld->bhcl', qc, ko).astype(jnp.float32) * scale
        s_own = s_own - slopes[None, :, None, None] * jnp.abs(dist_own)[None, None]
        s_own = jnp.where((dist_own >= 0)[None, None], s_own, neg)
        s_all = jnp.concatenate([s_sel.reshape(B, H, C, k_sel * BLK), s_own], axis=-1)
        p = jax.nn.softmax(s_all, axis=-1).astype(v.dtype)
        p_sel = p[..., :k_sel * BLK].reshape(B, H, C, k_sel, BLK)
        p_own = p[..., k_sel * BLK:]
        return (jnp.einsum('bhckl,bhckld->bhcd', p_sel, vs)
                + jnp.einsum('bhcl,bhld->bhcd', p_own, vo))

    outs = lax.map(chunk_fn, jnp.arange(s_pad // C))
    o = outs.transpose(1, 0, 3, 2, 4).reshape(B, s_pad, H * Dh)[:, :S]
    return o @ w_o


def conformer_conv(h, w_pw1, b_pw1, w_dw, b_dw, ln_g, ln_b, w_pw2, b_pw2):
    D = h.shape[-1]
    u = h @ w_pw1 + b_pw1
    a, g = jnp.split(u, 2, axis=-1)
    u = a * jax.nn.sigmoid(g)
    u = lax.conv_general_dilated(
        u, w_dw[:, None, :].astype(u.dtype), window_strides=(1,),
        padding=[(CONV_WIDTH - 1, 0)],
        dimension_numbers=('NWC', 'WIO', 'NWC'),
        feature_group_count=D) + b_dw
    u = jax.nn.silu(layer_norm(u, ln_g, ln_b))
    return u @ w_pw2 + b_pw2


def setup_inputs(seed: int = 0) -> dict:
    key = jax.random.key(seed)
    ks = jax.random.split(key, 24)
    D = D_MODEL
    nrm = lambda k, shape, fan_in: jax.random.normal(k, shape, jnp.float32) * (fan_in ** -0.5)
    small = lambda k, shape, s: jax.random.normal(k, shape, jnp.float32) * s
    return {
        "x": jax.random.normal(ks[0], (BATCH, SEQ, D), jnp.float32),
        "c": jax.random.normal(ks[1], (BATCH, D), jnp.float32),
        "norm_g": 1.0 + small(ks[2], (DEPTH, N_SUB, D), 0.02),
        "ada_w": nrm(ks[3], (DEPTH, D, N_SUB * 3 * D), D),
        "ada_b": small(ks[4], (DEPTH, N_SUB * 3 * D), 0.02),
        "ffn_w_in": nrm(ks[5], (DEPTH, 2, D, 2 * D_FF), D),
        "ffn_w_out": nrm(ks[6], (DEPTH, 2, D_FF, D), D_FF),
        "attn_w_qkv": nrm(ks[7], (N_ATTN, D, 3 * N_HEADS * HEAD_DIM), D),
        "attn_g_q": 1.0 + small(ks[8], (N_ATTN, HEAD_DIM), 0.02),
        "attn_g_k": 1.0 + small(ks[9], (N_ATTN, HEAD_DIM), 0.02),
        "attn_w_o": nrm(ks[10], (N_ATTN, N_HEADS * HEAD_DIM, D), N_HEADS * HEAD_DIM),
        "conv_w_pw1": nrm(ks[11], (N_CONV, D, 2 * D), D),
        "conv_b_pw1": small(ks[12], (N_CONV, 2 * D), 0.02),
        "conv_w_dw": nrm(ks[13], (N_CONV, CONV_WIDTH, D), CONV_WIDTH),
        "conv_b_dw": small(ks[14], (N_CONV, D), 0.02),
        "conv_ln_g": 1.0 + small(ks[15], (N_CONV, D), 0.02),
        "conv_ln_b": small(ks[16], (N_CONV, D), 0.02),
        "conv_w_pw2": nrm(ks[17], (N_CONV, D, D), D),
        "conv_b_pw2": small(ks[18], (N_CONV, D), 0.02),
    }


def reference(x, c, norm_g, ada_w, ada_b, ffn_w_in, ffn_w_out,
              attn_w_qkv, attn_g_q, attn_g_k, attn_w_o,
              conv_w_pw1, conv_b_pw1, conv_w_dw, conv_b_dw, conv_ln_g, conv_ln_b,
              conv_w_pw2, conv_b_pw2):
    B = x.shape[0]
    c_act = jax.nn.silu(c)
    for i in range(DEPTH):
        mod = (c_act @ ada_w[i] + ada_b[i]).reshape(B, N_SUB, 3, D_MODEL)
        shift, scl, gate = mod[:, :, 0, None, :], mod[:, :, 1, None, :], mod[:, :, 2, None, :]

        def pre(z, j):
            return rms_norm(z, norm_g[i, j]) * (1.0 + scl[:, j]) + shift[:, j]

        x = x + 0.5 * gate[:, 0] * swiglu_ffn(pre(x, 0), ffn_w_in[i, 0], ffn_w_out[i, 0])
        h = pre(x, 1)
        m = i // N_MIXERS
        if i % N_MIXERS == 0:
            y = moba_attention(h, attn_w_qkv[m], attn_g_q[m], attn_g_k[m], attn_w_o[m])
        else:
            y = conformer_conv(h, conv_w_pw1[m], conv_b_pw1[m], conv_w_dw[m], conv_b_dw[m],
                               conv_ln_g[m], conv_ln_b[m], conv_w_pw2[m], conv_b_pw2[m])
        x = x + gate[:, 1] * y
        x = x + 0.5 * gate[:, 2] * swiglu_ffn(pre(x, 2), ffn_w_in[i, 1], ffn_w_out[i, 1])
    return x
```

```python
import functools
import math

import numpy as np
import jax
import jax.numpy as jnp
from jax import lax
from jax.experimental import pallas as pl
from jax.experimental.pallas import tpu as pltpu

D_MODEL = 1024
N_HEADS = 16
HEAD_DIM = 64
MOBA_BLOCK = 256
MOBA_TOPK = 3
CONV_WIDTH = 31
D_FF = 2816
N_SUB = 3
EPS = 1e-6

LANES = 128
SUBLANES = 8
HEADS_PER_LANE_GROUP = LANES // HEAD_DIM
CONV_HALO = 32
MASK_BIAS = -1e30
LOG2E = math.log2(math.e)
VMEM_LIMIT = 56 * 1024 * 1024

F32 = jnp.float32
BF16 = jnp.bfloat16


def _cparams(sem):
    return pltpu.CompilerParams(dimension_semantics=sem, vmem_limit_bytes=VMEM_LIMIT)


def _resident(shape, index_map):
    return pl.BlockSpec(shape, index_map, pipeline_mode=pl.Buffered(1))


def _silu(x):
    return x * jax.nn.sigmoid(x)


def _modulated_norm(x, g, mod):
    ms = jnp.mean(x * x, axis=-1, keepdims=True)
    y = x * lax.rsqrt(ms + EPS) * g
    return y * (1.0 + mod[1:2, :]) + mod[0:1, :]


def _ada_kernel(ct_ref, w_ref, b_ref, o_ref, *, batch):
    ca = _silu(ct_ref[...])
    w = w_ref[0]
    for b in range(batch):
        o_ref[0, b:b + 1, :] = jnp.sum(w * ca[:, b:b + 1], axis=0, keepdims=True) + b_ref[0]


def _ada(c, ada_w, ada_b):
    depth, d, n = ada_w.shape
    batch = c.shape[0]
    tn = 1152
    return pl.pallas_call(
        functools.partial(_ada_kernel, batch=batch),
        out_shape=jax.ShapeDtypeStruct((depth, batch, n), F32),
        grid=(depth, n // tn),
        in_specs=[
            pl.BlockSpec((d, batch), lambda i, j: (0, 0)),
            pl.BlockSpec((1, d, tn), lambda i, j: (i, 0, j)),
            pl.BlockSpec((1, 1, tn), lambda i, j: (i, 0, j)),
        ],
        out_specs=pl.BlockSpec((1, batch, tn), lambda i, j: (i, 0, j)),
        compiler_params=_cparams(("arbitrary", "arbitrary")),
        name="adaln",
    )(c.T, ada_w, ada_b.reshape(depth, 1, n))


def _ffn_kernel(x_ref, g_ref, mod_ref, win_ref, wout_ref, o_ref):
    x = x_ref[0]
    mod = mod_ref[0]
    h = _modulated_norm(x, g_ref[...], mod).astype(BF16)
    u = jnp.dot(h, win_ref[...], preferred_element_type=F32)
    act = (_silu(u[:, :D_FF]) * u[:, D_FF:]).astype(BF16)
    y = jnp.dot(act, wout_ref[...], preferred_element_type=F32)
    o_ref[0] = x + (0.5 * mod[2:3, :]) * y


def _ffn(x, g, mod, w_in, w_out, tm=512):
    b, s, d = x.shape
    return pl.pallas_call(
        _ffn_kernel,
        out_shape=jax.ShapeDtypeStruct(x.shape, F32),
        grid=(b, s // tm),
        in_specs=[
            pl.BlockSpec((1, tm, d), lambda i, j: (i, j, 0)),
            pl.BlockSpec((1, d), lambda i, j: (0, 0)),
            pl.BlockSpec((1, 3, d), lambda i, j: (i, 0, 0)),
            _resident((d, 2 * D_FF), lambda i, j: (0, 0)),
            _resident((D_FF, d), lambda i, j: (0, 0)),
        ],
        out_specs=pl.BlockSpec((1, tm, d), lambda i, j: (i, j, 0)),
        compiler_params=_cparams(("parallel", "parallel")),
        name="ffn",
    )(x, g.reshape(1, d), mod, w_in.astype(BF16), w_out.astype(BF16))


def _head_rms_scale(t, e_ref, et_ref):
    ssq = jnp.dot((t * t).astype(BF16), e_ref[...], preferred_element_type=F32)
    r = lax.rsqrt(ssq * (1.0 / HEAD_DIM) + EPS)
    r_hi = r.astype(BF16)
    r_lo = (r - r_hi.astype(F32)).astype(BF16)
    return jnp.dot(jnp.concatenate([r_hi, r_lo], axis=1), et_ref[...], preferred_element_type=F32)


def _qkv_kernel(x_ref, g_ref, mod_ref, w_ref, gq_ref, gk_ref, e_ref, et_ref,
                q_ref, k_ref, v_ref, kbar_ref, *, blocks_per_tile):
    d = D_MODEL
    h = _modulated_norm(x_ref[0], g_ref[...], mod_ref[0]).astype(BF16)
    qkv = jnp.dot(h, w_ref[...], preferred_element_type=F32)
    q, k, v = qkv[:, :d], qkv[:, d:2 * d], qkv[:, 2 * d:]
    qn = q * _head_rms_scale(q, e_ref, et_ref) * gq_ref[...]
    kn = k * _head_rms_scale(k, e_ref, et_ref) * gk_ref[...]
    q_ref[0] = qn.astype(BF16)
    k_ref[0] = kn.astype(BF16)
    v_ref[0] = v.astype(BF16)
    for t in range(blocks_per_tile):
        blk = kn[t * MOBA_BLOCK:(t + 1) * MOBA_BLOCK, :]
        kbar_ref[0, t] = jnp.mean(blk, axis=0, keepdims=True)


def _qkv(x, g, mod, w_qkv, gq_row, gk_row, e_mat, et_mat, tm=512):
    b, s, d = x.shape
    nb = s // MOBA_BLOCK
    bpt = tm // MOBA_BLOCK
    tok = jax.ShapeDtypeStruct((b, s, d), BF16)
    tok_spec = pl.BlockSpec((1, tm, d), lambda i, j: (i, j, 0))
    return pl.pallas_call(
        functools.partial(_qkv_kernel, blocks_per_tile=bpt),
        out_shape=(tok, tok, tok, jax.ShapeDtypeStruct((b, nb, 1, d), F32)),
        grid=(b, s // tm),
        in_specs=[
            tok_spec,
            pl.BlockSpec((1, d), lambda i, j: (0, 0)),
            pl.BlockSpec((1, 3, d), lambda i, j: (i, 0, 0)),
            _resident((d, 3 * d), lambda i, j: (0, 0)),
            pl.BlockSpec((1, d), lambda i, j: (0, 0)),
            pl.BlockSpec((1, d), lambda i, j: (0, 0)),
            _resident((d, LANES), lambda i, j: (0, 0)),
            _resident((2 * LANES, d), lambda i, j: (0, 0)),
        ],
        out_specs=(tok_spec, tok_spec, tok_spec,
                   pl.BlockSpec((1, bpt, 1, d), lambda i, j: (i, j, 0, 0))),
        compiler_params=_cparams(("parallel", "parallel")),
        name="qkv",
    )(x, g.reshape(1, d), mod, w_qkv.astype(BF16), gq_row, gk_row, e_mat, et_mat)


def _gate_kernel(q_ref, kbd_ref, ext_ref, sel_ref):
    own = pl.program_id(1)
    rows = q_ref.shape[1]
    lane = lax.broadcasted_iota(jnp.int32, (rows, LANES), 1)
    lane_f = lane.astype(F32)
    past = lane < own
    heads_per_chunk = 2 * LANES // HEAD_DIM
    for c in range(D_MODEL // (2 * LANES)):
        qc = q_ref[0, :, c * 2 * LANES:(c + 1) * 2 * LANES]
        gates = jnp.dot(qc, kbd_ref[0, c], preferred_element_type=F32)
        for hh in range(heads_per_chunk):
            head = c * heads_per_chunk + hh
            g = jnp.where(past, gates[:, hh * LANES:(hh + 1) * LANES], -jnp.inf)
            taken = jnp.zeros(g.shape, jnp.bool_)
            for _ in range(MOBA_TOPK):
                gm = jnp.where(taken, -jnp.inf, g)
                m = jnp.max(gm, axis=-1, keepdims=True)
                cand = jnp.logical_and(gm == m, jnp.logical_not(taken))
                idx = jnp.min(jnp.where(cand, lane_f, float(LANES)), axis=-1, keepdims=True)
                taken = jnp.logical_or(taken, lane_f == idx)
            chosen = jnp.logical_and(taken, past)
            bias = jnp.where(chosen, 0.0, MASK_BIAS)
            out = jnp.where(lane < HEAD_DIM, bias, ext_ref[head])
            sel_ref[0, :, head * LANES:(head + 1) * LANES] = out.astype(BF16)


def _gate(q, kbd, ext):
    b, s, d = q.shape
    nb = s // MOBA_BLOCK
    nchunk = d // (2 * LANES)
    return pl.pallas_call(
        _gate_kernel,
        out_shape=jax.ShapeDtypeStruct((b, s, N_HEADS * LANES), BF16),
        grid=(b, nb),
        in_specs=[
            pl.BlockSpec((1, MOBA_BLOCK, d), lambda i, j: (i, j, 0)),
            pl.BlockSpec((1, nchunk, 2 * LANES, 4 * LANES), lambda i, j: (i, 0, 0, 0)),
            pl.BlockSpec((N_HEADS, 1, LANES), lambda i, j: (0, 0, 0)),
        ],
        out_specs=pl.BlockSpec((1, MOBA_BLOCK, N_HEADS * LANES), lambda i, j: (i, j, 0)),
        compiler_params=_cparams(("parallel", "parallel")),
        name="moba_gate",
    )(q, kbd, ext)


def _attn_kernel(q_ref, sel_ref, k_ref, v_ref, kx_ref, o_ref, qa_scr, m_scr, l_scr, acc_scr):
    i = pl.program_id(2)
    blk = MOBA_BLOCK
    lane = lax.broadcasted_iota(jnp.int32, (blk, LANES), 1)
    qp = q_ref[0]
    zero = jnp.zeros_like(qp)
    row = lax.broadcasted_iota(jnp.int32, (blk, blk), 0)
    col = lax.broadcasted_iota(jnp.int32, (blk, blk), 1)
    causal = row >= col

    def kv_block(j):
        start = pl.multiple_of(j * blk, blk)
        ka = jnp.concatenate([k_ref[0, pl.ds(start, blk), :], kx_ref[pl.ds(start, blk), :]], axis=1)
        return ka, v_ref[0, pl.ds(start, blk), :]

    ka, vj = kv_block(i)
    for h in range(HEADS_PER_LANE_GROUP):
        in_head = (lane >= h * HEAD_DIM) & (lane < (h + 1) * HEAD_DIM)
        qh = jnp.where(in_head, qp, zero)
        sx = sel_ref[0, :, h * LANES:(h + 1) * LANES]
        qa_scr[h] = jnp.concatenate([qh, sx], axis=1)
        qd = jnp.concatenate([qh, jnp.where(lane < HEAD_DIM, jnp.zeros_like(sx), sx)], axis=1)
        s = lax.dot_general(qd, ka, (((1,), (1,)), ((), ())), preferred_element_type=F32)
        s = jnp.where(causal, s, MASK_BIAS)
        m = jnp.max(s, axis=-1, keepdims=True)
        p = jnp.exp2(s - m)
        m_scr[h] = jnp.broadcast_to(m, (blk, LANES))
        l_scr[h] = jnp.broadcast_to(jnp.sum(p, axis=-1, keepdims=True), (blk, LANES))
        acc_scr[h] = jnp.dot(p.astype(BF16), vj, preferred_element_type=F32)

    def body(j, carry):
        ka, vj = kv_block(j)
        for h in range(HEADS_PER_LANE_GROUP):
            s = lax.dot_general(qa_scr[h], ka, (((1,), (1,)), ((), ())), preferred_element_type=F32)
            m_prev = m_scr[h]
            m_new = jnp.maximum(m_prev, jnp.max(s, axis=-1, keepdims=True))
            alpha = jnp.exp2(m_prev - m_new)
            p = jnp.exp2(s - m_new[:, :1])
            l_scr[h] = alpha * l_scr[h] + jnp.sum(p, axis=-1, keepdims=True)
            acc_scr[h] = alpha * acc_scr[h] + jnp.dot(p.astype(BF16), vj, preferred_element_type=F32)
            m_scr[h] = m_new
        return carry

    lax.fori_loop(0, i, body, 0)
    out = jnp.where(lane < HEAD_DIM, acc_scr[0] / l_scr[0], acc_scr[1] / l_scr[1])
    o_ref[0] = out.astype(BF16)


def _attn(q, selx, k, v, kx):
    b, s, d = q.shape
    nb = s // MOBA_BLOCK
    pairs = d // LANES
    blk = MOBA_BLOCK
    return pl.pallas_call(
        _attn_kernel,
        out_shape=jax.ShapeDtypeStruct((b, s, d), BF16),
        grid=(b, pairs, nb),
        in_specs=[
            pl.BlockSpec((1, blk, LANES), lambda bi, p, i: (bi, i, p)),
            pl.BlockSpec((1, blk, HEADS_PER_LANE_GROUP * LANES), lambda bi, p, i: (bi, i, p)),
            pl.BlockSpec((1, s, LANES), lambda bi, p, i: (bi, 0, p)),
            pl.BlockSpec((1, s, LANES), lambda bi, p, i: (bi, 0, p)),
            _resident((s, LANES), lambda bi, p, i: (0, 0)),
        ],
        out_specs=pl.BlockSpec((1, blk, LANES), lambda bi, p, i: (bi, i, p)),
        scratch_shapes=[
            pltpu.VMEM((HEADS_PER_LANE_GROUP, blk, 2 * LANES), BF16),
            pltpu.VMEM((HEADS_PER_LANE_GROUP, blk, LANES), F32),
            pltpu.VMEM((HEADS_PER_LANE_GROUP, blk, LANES), F32),
            pltpu.VMEM((HEADS_PER_LANE_GROUP, blk, LANES), F32),
        ],
        compiler_params=_cparams(("parallel", "parallel", "arbitrary")),
        name="moba_attn",
    )(q, selx, k, v, kx)


def _proj_kernel(o_ref, x_ref, mod_ref, w_ref, y_ref):
    y = jnp.dot(o_ref[0], w_ref[...], preferred_element_type=F32)
    y_ref[0] = x_ref[0] + mod_ref[0][2:3, :] * y


def _proj(o, x, mod, w, tm=512):
    b, s, d = x.shape
    tok = lambda i, j: (i, j, 0)
    return pl.pallas_call(
        _proj_kernel,
        out_shape=jax.ShapeDtypeStruct(x.shape, F32),
        grid=(b, s // tm),
        in_specs=[
            pl.BlockSpec((1, tm, d), tok),
            pl.BlockSpec((1, tm, d), tok),
            pl.BlockSpec((1, 3, d), lambda i, j: (i, 0, 0)),
            _resident((d, d), lambda i, j: (0, 0)),
        ],
        out_specs=pl.BlockSpec((1, tm, d), tok),
        compiler_params=_cparams(("parallel", "parallel")),
        name="attn_out_proj",
    )(o, x, mod, w.astype(BF16))


def _conv_kernel(x_ref, g_ref, mod_ref, w1_ref, b1_ref, wdw_ref, bdw_ref, lng_ref, lnb_ref,
                 w2_ref, b2_ref, o_ref, u_scr, c_scr, *, row_chunk):
    d = D_MODEL
    tm = x_ref.shape[1]
    x = x_ref[0]
    mod = mod_ref[0]

    @pl.when(pl.program_id(1) == 0)
    def _():
        u_scr[0:CONV_HALO, :] = jnp.zeros((CONV_HALO, d), F32)

    h = _modulated_norm(x, g_ref[...], mod).astype(BF16)
    u = jnp.dot(h, w1_ref[...], preferred_element_type=F32) + b1_ref[...]
    u_scr[CONV_HALO:CONV_HALO + tm, :] = u[:, :d] * jax.nn.sigmoid(u[:, d:])

    base = CONV_HALO - (CONV_WIDTH - 1)

    def chunk(r, carry):
        r0 = pl.multiple_of(r * row_chunk, row_chunk)
        for lc in range(d // LANES):
            ls = slice(lc * LANES, (lc + 1) * LANES)
            win = u_scr[pl.ds(r0, row_chunk + CONV_HALO), ls]
            acc = jnp.broadcast_to(bdw_ref[:, ls], (row_chunk, LANES))
            for shift in range(SUBLANES):
                taps = [w for w in range(CONV_WIDTH) if (base + w) % SUBLANES == shift]
                span = max((base + w) // SUBLANES for w in taps) * SUBLANES + row_chunk
                shifted = win[shift:shift + span]
                for w in taps:
                    a0 = (base + w) // SUBLANES * SUBLANES
                    acc = acc + shifted[a0:a0 + row_chunk] * wdw_ref[w:w + 1, ls]
            c_scr[pl.ds(r0, row_chunk), ls] = acc
        return carry

    lax.fori_loop(0, tm // row_chunk, chunk, 0)
    u_scr[0:CONV_HALO, :] = u_scr[tm:tm + CONV_HALO, :]

    c = c_scr[...]
    mu = jnp.mean(c, axis=-1, keepdims=True)
    cc = c - mu
    var = jnp.mean(cc * cc, axis=-1, keepdims=True)
    z = _silu(cc * lax.rsqrt(var + EPS) * lng_ref[...] + lnb_ref[...]).astype(BF16)
    y = jnp.dot(z, w2_ref[...], preferred_element_type=F32) + b2_ref[...]
    o_ref[0] = x + mod[2:3, :] * y


def _conv(x, g, mod, w1, b1, wdw, bdw, lng, lnb, w2, b2, tm=512, row_chunk=64):
    b, s, d = x.shape
    tok = lambda i, j: (i, j, 0)
    vec = lambda n: pl.BlockSpec((1, n), lambda i, j: (0, 0))
    return pl.pallas_call(
        functools.partial(_conv_kernel, row_chunk=row_chunk),
        out_shape=jax.ShapeDtypeStruct(x.shape, F32),
        grid=(b, s // tm),
        in_specs=[
            pl.BlockSpec((1, tm, d), tok),
            vec(d),
            pl.BlockSpec((1, 3, d), lambda i, j: (i, 0, 0)),
            _resident((d, 2 * d), lambda i, j: (0, 0)),
            vec(2 * d),
            pl.BlockSpec((CONV_WIDTH, d), lambda i, j: (0, 0)),
            vec(d), vec(d), vec(d),
            _resident((d, d), lambda i, j: (0, 0)),
            vec(d),
        ],
        out_specs=pl.BlockSpec((1, tm, d), tok),
        scratch_shapes=[pltpu.VMEM((CONV_HALO + tm, d), F32), pltpu.VMEM((tm, d), F32)],
        compiler_params=_cparams(("parallel", "arbitrary")),
        name="conformer_conv",
    )(x, g.reshape(1, d), mod, w1.astype(BF16), b1.reshape(1, 2 * d), wdw, bdw.reshape(1, d),
      lng.reshape(1, d), lnb.reshape(1, d), w2.astype(BF16), b2.reshape(1, d))


def _bf16_pieces(x, n=3):
    out = []
    rest = np.float64(x)
    for _ in range(n):
        p = np.float64(np.asarray(rest, np.float32).astype(jnp.bfloat16).astype(np.float32))
        out.append(float(p))
        rest = rest - p
    return out


def _alibi_query_rows():
    ext = np.zeros((N_HEADS, 1, LANES), np.float32)
    for h in range(N_HEADS):
        slope = np.float32(2.0 ** (-8.0 * (h + 1) / N_HEADS))
        pieces = _bf16_pieces(float(slope) * LOG2E)
        ext[h, 0, HEAD_DIM:HEAD_DIM + 3] = pieces
        ext[h, 0, HEAD_DIM + 3:HEAD_DIM + 6] = [MOBA_BLOCK * p for p in pieces]
    return jnp.asarray(ext)


def _key_side_table(s):
    pos = np.arange(s)
    blk, off = pos // MOBA_BLOCK, pos % MOBA_BLOCK
    kx = np.zeros((s, LANES), np.float32)
    kx[pos, blk] = 1.0
    kx[:, HEAD_DIM:HEAD_DIM + 3] = off[:, None]
    kx[:, HEAD_DIM + 3:HEAD_DIM + 6] = blk[:, None]
    return jnp.asarray(kx, dtype=BF16)


def _head_indicator():
    e = np.zeros((D_MODEL, LANES), np.float32)
    e[np.arange(D_MODEL), np.arange(D_MODEL) // HEAD_DIM] = 1.0
    et = np.concatenate([e.T, e.T], axis=0)
    return jnp.asarray(e, dtype=BF16), jnp.asarray(et, dtype=BF16)


def _block_diag_kbar(kbar):
    b, nb = kbar.shape[0], kbar.shape[1]
    kb = kbar.reshape(b, nb, N_HEADS, HEAD_DIM).transpose(0, 2, 3, 1)
    kb = jnp.pad(kb, ((0, 0), (0, 0), (0, 0), (0, LANES - nb)))
    hpc = 2 * LANES // HEAD_DIM
    kb = kb.reshape(b, N_HEADS // hpc, hpc, HEAD_DIM, LANES)
    eye = jnp.eye(hpc, dtype=kb.dtype)
    out = jnp.einsum('bchdl,hg->bchdgl', kb, eye)
    return out.reshape(b, N_HEADS // hpc, hpc * HEAD_DIM, hpc * LANES).astype(BF16)


def _moba_mixer(x, g, mod, w_qkv, g_q, g_k, w_o):
    s = x.shape[1]
    assert s % MOBA_BLOCK == 0 and s // MOBA_BLOCK <= HEAD_DIM
    e_mat, et_mat = _head_indicator()
    q_scale = LOG2E / math.sqrt(HEAD_DIM)
    gq_row = (jnp.tile(g_q, N_HEADS) * q_scale).reshape(1, D_MODEL)
    gk_row = jnp.tile(g_k, N_HEADS).reshape(1, D_MODEL)
    q, k, v, kbar = _qkv(x, g, mod, w_qkv, gq_row, gk_row, e_mat, et_mat)
    selx = _gate(q, _block_diag_kbar(kbar), _alibi_query_rows())
    o = _attn(q, selx, k, v, _key_side_table(s))
    return _proj(o, x, mod, w_o)


def kernel(x, c, norm_g, ada_w, ada_b, ffn_w_in, ffn_w_out, attn_w_qkv, attn_g_q, attn_g_k, attn_w_o,
           conv_w_pw1, conv_b_pw1, conv_w_dw, conv_b_dw, conv_ln_g, conv_ln_b, conv_w_pw2, conv_b_pw2):
    depth = norm_g.shape[0]
    b = x.shape[0]
    mods = _ada(c, ada_w, ada_b).reshape(depth, b, N_SUB, 3, D_MODEL)
    for i in range(depth):
        mod = lambda j: mods[i, :, j]
        x = _ffn(x, norm_g[i, 0], mod(0), ffn_w_in[i, 0], ffn_w_out[i, 0])
        m = i // 2
        if i % 2 == 0:
            x = _moba_mixer(x, norm_g[i, 1], mod(1), attn_w_qkv[m], attn_g_q[m], attn_g_k[m], attn_w_o[m])
        else:
            x = _conv(x, norm_g[i, 1], mod(1), conv_w_pw1[m], conv_b_pw1[m], conv_w_dw[m], conv_b_dw[m],
                      conv_ln_g[m], conv_ln_b[m], conv_w_pw2[m], conv_b_pw2[m])
        x = _ffn(x, norm_g[i, 2], mod(2), ffn_w_in[i, 1], ffn_w_out[i, 1])
    return x
```

```python
import functools
import math

import numpy as np
import jax
import jax.numpy as jnp
from jax import lax
from jax.experimental import pallas as pl
from jax.experimental.pallas import tpu as pltpu

D_MODEL = 1024
N_HEADS = 16
HEAD_DIM = 64
MOBA_BLOCK = 256
MOBA_TOPK = 3
CONV_WIDTH = 31
D_FF = 2816
N_SUB = 3
EPS = 1e-6

LANES = 128
SUBLANES = 8
HEADS_PER_LANE_GROUP = LANES // HEAD_DIM
CONV_HALO = 32
MASK_BIAS = -1e30
LOG2E = math.log2(math.e)
VMEM_LIMIT = 56 * 1024 * 1024

F32 = jnp.float32
BF16 = jnp.bfloat16


def _cparams(sem):
    return pltpu.CompilerParams(dimension_semantics=sem, vmem_limit_bytes=VMEM_LIMIT)


def _resident(shape, index_map):
    return pl.BlockSpec(shape, index_map, pipeline_mode=pl.Buffered(1))


def _silu(x):
    return x * jax.nn.sigmoid(x)


def _modulated_norm(x, g, mod):
    ms = jnp.mean(x * x, axis=-1, keepdims=True)
    y = x * lax.rsqrt(ms + EPS) * g
    return y * (1.0 + mod[1:2, :]) + mod[0:1, :]


def _ada_kernel(ct_ref, w_ref, b_ref, o_ref, *, batch):
    ca = _silu(ct_ref[...])
    w = w_ref[0]
    for b in range(batch):
        o_ref[0, b:b + 1, :] = jnp.sum(w * ca[:, b:b + 1], axis=0, keepdims=True) + b_ref[0]


def _ada(c, ada_w, ada_b):
    depth, d, n = ada_w.shape
    batch = c.shape[0]
    tn = 1152
    return pl.pallas_call(
        functools.partial(_ada_kernel, batch=batch),
        out_shape=jax.ShapeDtypeStruct((depth, batch, n), F32),
        grid=(depth, n // tn),
        in_specs=[
            pl.BlockSpec((d, batch), lambda i, j: (0, 0)),
            pl.BlockSpec((1, d, tn), lambda i, j: (i, 0, j)),
            pl.BlockSpec((1, 1, tn), lambda i, j: (i, 0, j)),
        ],
        out_specs=pl.BlockSpec((1, batch, tn), lambda i, j: (i, 0, j)),
        compiler_params=_cparams(("arbitrary", "arbitrary")),
        name="adaln",
    )(c.T, ada_w, ada_b.reshape(depth, 1, n))


def _ffn_kernel(x_ref, g_ref, mod_ref, win_ref, wout_ref, o_ref):
    x = x_ref[0]
    mod = mod_ref[0]
    h = _modulated_norm(x, g_ref[...], mod).astype(BF16)
    u = jnp.dot(h, win_ref[...], preferred_element_type=F32)
    act = (_silu(u[:, :D_FF]) * u[:, D_FF:]).astype(BF16)
    y = jnp.dot(act, wout_ref[...], preferred_element_type=F32)
    o_ref[0] = x + (0.5 * mod[2:3, :]) * y


def _ffn(x, g, mod, w_in, w_out, tm=512):
    b, s, d = x.shape
    return pl.pallas_call(
        _ffn_kernel,
        out_shape=jax.ShapeDtypeStruct(x.shape, F32),
        grid=(b, s // tm),
        in_specs=[
            pl.BlockSpec((1, tm, d), lambda i, j: (i, j, 0)),
            pl.BlockSpec((1, d), lambda i, j: (0, 0)),
            pl.BlockSpec((1, 3, d), lambda i, j: (i, 0, 0)),
            _resident((d, 2 * D_FF), lambda i, j: (0, 0)),
            _resident((D_FF, d), lambda i, j: (0, 0)),
        ],
        out_specs=pl.BlockSpec((1, tm, d), lambda i, j: (i, j, 0)),
        compiler_params=_cparams(("parallel", "parallel")),
        name="ffn",
    )(x, g.reshape(1, d), mod, w_in.astype(BF16), w_out.astype(BF16))


def _head_rms_scale(t, e_ref, et_ref):
    ssq = jnp.dot((t * t).astype(BF16), e_ref[...], preferred_element_type=F32)
    r = lax.rsqrt(ssq * (1.0 / HEAD_DIM) + EPS)
    r_hi = r.astype(BF16)
    r_lo = (r - r_hi.astype(F32)).astype(BF16)
    return jnp.dot(jnp.concatenate([r_hi, r_lo], axis=1), et_ref[...], preferred_element_type=F32)


def _qkv_kernel(x_ref, g_ref, mod_ref, w_ref, gq_ref, gk_ref, e_ref, et_ref,
                q_ref, k_ref, v_ref, kbar_ref, *, blocks_per_tile):
    d = D_MODEL
    h = _modulated_norm(x_ref[0], g_ref[...], mod_ref[0]).astype(BF16)
    qkv = jnp.dot(h, w_ref[...], preferred_element_type=F32)
    q, k, v = qkv[:, :d], qkv[:, d:2 * d], qkv[:, 2 * d:]
    qn = q * _head_rms_scale(q, e_ref, et_ref) * gq_ref[...]
    kn = k * _head_rms_scale(k, e_ref, et_ref) * gk_ref[...]
    q_ref[0] = qn.astype(BF16)
    k_ref[0] = kn.astype(BF16)
    v_ref[0] = v.astype(BF16)
    for t in range(blocks_per_tile):
        blk = kn[t * MOBA_BLOCK:(t + 1) * MOBA_BLOCK, :]
        kbar_ref[0, t] = jnp.mean(blk, axis=0, keepdims=True)


def _qkv(x, g, mod, w_qkv, gq_row, gk_row, e_mat, et_mat, tm=512):
    b, s, d = x.shape
    nb = s // MOBA_BLOCK
    bpt = tm // MOBA_BLOCK
    tok = jax.ShapeDtypeStruct((b, s, d), BF16)
    tok_spec = pl.BlockSpec((1, tm, d), lambda i, j: (i, j, 0))
    return pl.pallas_call(
        functools.partial(_qkv_kernel, blocks_per_tile=bpt),
        out_shape=(tok, tok, tok, jax.ShapeDtypeStruct((b, nb, 1, d), F32)),
        grid=(b, s // tm),
        in_specs=[
            tok_spec,
            pl.BlockSpec((1, d), lambda i, j: (0, 0)),
            pl.BlockSpec((1, 3, d), lambda i, j: (i, 0, 0)),
            _resident((d, 3 * d), lambda i, j: (0, 0)),
            pl.BlockSpec((1, d), lambda i, j: (0, 0)),
            pl.BlockSpec((1, d), lambda i, j: (0, 0)),
            _resident((d, LANES), lambda i, j: (0, 0)),
            _resident((2 * LANES, d), lambda i, j: (0, 0)),
        ],
        out_specs=(tok_spec, tok_spec, tok_spec,
                   pl.BlockSpec((1, bpt, 1, d), lambda i, j: (i, j, 0, 0))),
        compiler_params=_cparams(("parallel", "parallel")),
        name="qkv",
    )(x, g.reshape(1, d), mod, w_qkv.astype(BF16), gq_row, gk_row, e_mat, et_mat)


def _gate_kernel(q_ref, kbd_ref, ext_ref, sel_ref):
    own = pl.program_id(1)
    rows = q_ref.shape[1]
    lane = lax.broadcasted_iota(jnp.int32, (rows, LANES), 1)
    lane_f = lane.astype(F32)
    past = lane < own
    heads_per_chunk = 2 * LANES // HEAD_DIM
    for c in range(D_MODEL // (2 * LANES)):
        qc = q_ref[0, :, c * 2 * LANES:(c + 1) * 2 * LANES]
        gates = jnp.dot(qc, kbd_ref[0, c], preferred_element_type=F32)
        for hh in range(heads_per_chunk):
            head = c * heads_per_chunk + hh
            g = jnp.where(past, gates[:, hh * LANES:(hh + 1) * LANES], -jnp.inf)
            taken = jnp.zeros(g.shape, jnp.bool_)
            for _ in range(MOBA_TOPK):
                gm = jnp.where(taken, -jnp.inf, g)
                m = jnp.max(gm, axis=-1, keepdims=True)
                cand = jnp.logical_and(gm == m, jnp.logical_not(taken))
                idx = jnp.min(jnp.where(cand, lane_f, float(LANES)), axis=-1, keepdims=True)
                taken = jnp.logical_or(taken, lane_f == idx)
            chosen = jnp.logical_and(taken, past)
            bias = jnp.where(chosen, 0.0, MASK_BIAS)
            out = jnp.where(lane < HEAD_DIM, bias, ext_ref[head])
            sel_ref[0, :, head * LANES:(head + 1) * LANES] = out.astype(BF16)


def _gate(q, kbd, ext):
    b, s, d = q.shape
    nb = s // MOBA_BLOCK
    nchunk = d // (2 * LANES)
    return pl.pallas_call(
        _gate_kernel,
        out_shape=jax.ShapeDtypeStruct((b, s, N_HEADS * LANES), BF16),
        grid=(b, nb),
        in_specs=[
            pl.BlockSpec((1, MOBA_BLOCK, d), lambda i, j: (i, j, 0)),
            pl.BlockSpec((1, nchunk, 2 * LANES, 4 * LANES), lambda i, j: (i, 0, 0, 0)),
            pl.BlockSpec((N_HEADS, 1, LANES), lambda i, j: (0, 0, 0)),
        ],
        out_specs=pl.BlockSpec((1, MOBA_BLOCK, N_HEADS * LANES), lambda i, j: (i, j, 0)),
        compiler_params=_cparams(("parallel", "parallel")),
        name="moba_gate",
    )(q, kbd, ext)


def _attn_kernel(q_ref, sel_ref, k_ref, v_ref, kx_ref, o_ref, qa_scr, qd_scr, m_scr, l_scr, acc_scr,
                 *, group, kv_blocks):
    step = pl.program_id(2)
    tq = group * MOBA_BLOCK
    tkv = kv_blocks * MOBA_BLOCK
    chunks = group // kv_blocks
    heads = range(HEADS_PER_LANE_GROUP)
    rows = HEADS_PER_LANE_GROUP * tq
    lane = lax.broadcasted_iota(jnp.int32, (tq, LANES), 1)
    contract_last = (((1,), (1,)), ((), ()))

    def visit(chunk, lhs_scr, mask=None):
        start = pl.multiple_of(chunk * tkv, tkv)
        ka = jnp.concatenate([k_ref[0, pl.ds(start, tkv), :], kx_ref[pl.ds(start, tkv), :]], axis=1)
        s = lax.dot_general(lhs_scr[...], ka, contract_last, preferred_element_type=F32)
        if mask is not None:
            s = jnp.where(mask, s, MASK_BIAS)
        m_prev = m_scr[...]
        m_new = jnp.maximum(m_prev, jnp.max(s, axis=-1, keepdims=True))
        alpha = jnp.exp2(m_prev - m_new)
        p = jnp.exp2(s - m_new[:, :1])
        l_scr[...] = alpha * l_scr[...] + sum(p[:, c * LANES:(c + 1) * LANES] for c in range(tkv // LANES))
        pv = jnp.dot(p.astype(BF16), v_ref[0, pl.ds(start, tkv), :], preferred_element_type=F32)
        acc_scr[...] = alpha * acc_scr[...] + pv
        m_scr[...] = m_new

    qp = q_ref[0]
    row_block = lax.shift_right_logical(lax.broadcasted_iota(jnp.int32, (tq, LANES), 0),
                                        MOBA_BLOCK.bit_length() - 1) + step * group
    for h in heads:
        in_head = (lane >= h * HEAD_DIM) & (lane < (h + 1) * HEAD_DIM)
        qh = jnp.where(in_head, qp, jnp.zeros_like(qp))
        sx = sel_ref[0, :, h * LANES:(h + 1) * LANES]
        qa_scr[h * tq:(h + 1) * tq, :] = jnp.concatenate([qh, sx], axis=1)
        qd_scr[h * tq:(h + 1) * tq, :] = jnp.concatenate(
            [qh, jnp.where(lane == row_block, jnp.zeros_like(sx), sx)], axis=1)
    m_scr[...] = jnp.full((rows, LANES), MASK_BIAS, F32)
    l_scr[...] = jnp.zeros((rows, LANES), F32)
    acc_scr[...] = jnp.zeros((rows, LANES), F32)

    query = lax.rem(lax.broadcasted_iota(jnp.int32, (rows, tkv), 0), tq)
    col = lax.broadcasted_iota(jnp.int32, (rows, tkv), 1)
    for u in range(chunks):
        visit(step * chunks + u, qd_scr, query >= col + u * tkv)

    def body(it, carry):
        for u in range(chunks):
            visit(it * chunks + u, qa_scr)
        return carry

    lax.fori_loop(0, step, body, 0)
    out = acc_scr[...] / jnp.sum(l_scr[...], axis=-1, keepdims=True)
    o_ref[0] = jnp.where(lane < HEAD_DIM, out[:tq], out[tq:]).astype(BF16)


def _attn(q, selx, k, v, kx, group=4, kv_blocks=2):
    b, s, d = q.shape
    pairs = d // LANES
    tq = group * MOBA_BLOCK
    return pl.pallas_call(
        functools.partial(_attn_kernel, group=group, kv_blocks=kv_blocks),
        out_shape=jax.ShapeDtypeStruct((b, s, d), BF16),
        grid=(b, pairs, s // tq),
        in_specs=[
            pl.BlockSpec((1, tq, LANES), lambda bi, p, i: (bi, i, p)),
            pl.BlockSpec((1, tq, HEADS_PER_LANE_GROUP * LANES), lambda bi, p, i: (bi, i, p)),
            pl.BlockSpec((1, s, LANES), lambda bi, p, i: (bi, 0, p)),
            pl.BlockSpec((1, s, LANES), lambda bi, p, i: (bi, 0, p)),
            _resident((s, LANES), lambda bi, p, i: (0, 0)),
        ],
        out_specs=pl.BlockSpec((1, tq, LANES), lambda bi, p, i: (bi, i, p)),
        scratch_shapes=[
            pltpu.VMEM((HEADS_PER_LANE_GROUP * tq, 2 * LANES), BF16),
            pltpu.VMEM((HEADS_PER_LANE_GROUP * tq, 2 * LANES), BF16),
            pltpu.VMEM((HEADS_PER_LANE_GROUP * tq, LANES), F32),
            pltpu.VMEM((HEADS_PER_LANE_GROUP * tq, LANES), F32),
            pltpu.VMEM((HEADS_PER_LANE_GROUP * tq, LANES), F32),
        ],
        compiler_params=_cparams(("parallel", "parallel", "arbitrary")),
        name="moba_attn",
    )(q, selx, k, v, kx)


def _proj_kernel(o_ref, x_ref, mod_ref, w_ref, y_ref):
    y = jnp.dot(o_ref[0], w_ref[...], preferred_element_type=F32)
    y_ref[0] = x_ref[0] + mod_ref[0][2:3, :] * y


def _proj(o, x, mod, w, tm=512):
    b, s, d = x.shape
    tok = lambda i, j: (i, j, 0)
    return pl.pallas_call(
        _proj_kernel,
        out_shape=jax.ShapeDtypeStruct(x.shape, F32),
        grid=(b, s // tm),
        in_specs=[
            pl.BlockSpec((1, tm, d), tok),
            pl.BlockSpec((1, tm, d), tok),
            pl.BlockSpec((1, 3, d), lambda i, j: (i, 0, 0)),
            _resident((d, d), lambda i, j: (0, 0)),
        ],
        out_specs=pl.BlockSpec((1, tm, d), tok),
        compiler_params=_cparams(("parallel", "parallel")),
        name="attn_out_proj",
    )(o, x, mod, w.astype(BF16))


def _conv_kernel(x_ref, g_ref, mod_ref, w1_ref, b1_ref, wdw_ref, bdw_ref, lng_ref, lnb_ref,
                 w2_ref, b2_ref, o_ref, u_scr, c_scr, *, row_chunk):
    d = D_MODEL
    tm = x_ref.shape[1]
    x = x_ref[0]
    mod = mod_ref[0]

    @pl.when(pl.program_id(1) == 0)
    def _():
        u_scr[0:CONV_HALO, :] = jnp.zeros((CONV_HALO, d), F32)

    h = _modulated_norm(x, g_ref[...], mod).astype(BF16)
    u = jnp.dot(h, w1_ref[...], preferred_element_type=F32) + b1_ref[...]
    u_scr[CONV_HALO:CONV_HALO + tm, :] = u[:, :d] * jax.nn.sigmoid(u[:, d:])

    base = CONV_HALO - (CONV_WIDTH - 1)

    def chunk(r, carry):
        r0 = pl.multiple_of(r * row_chunk, row_chunk)
        for lc in range(d // LANES):
            ls = slice(lc * LANES, (lc + 1) * LANES)
            win = u_scr[pl.ds(r0, row_chunk + CONV_HALO), ls]
            acc = jnp.broadcast_to(bdw_ref[:, ls], (row_chunk, LANES))
            for shift in range(SUBLANES):
                taps = [w for w in range(CONV_WIDTH) if (base + w) % SUBLANES == shift]
                span = max((base + w) // SUBLANES for w in taps) * SUBLANES + row_chunk
                shifted = win[shift:shift + span]
                for w in taps:
                    a0 = (base + w) // SUBLANES * SUBLANES
                    acc = acc + shifted[a0:a0 + row_chunk] * wdw_ref[w:w + 1, ls]
            c_scr[pl.ds(r0, row_chunk), ls] = acc
        return carry

    lax.fori_loop(0, tm // row_chunk, chunk, 0)
    u_scr[0:CONV_HALO, :] = u_scr[tm:tm + CONV_HALO, :]

    c = c_scr[...]
    mu = jnp.mean(c, axis=-1, keepdims=True)
    cc = c - mu
    var = jnp.mean(cc * cc, axis=-1, keepdims=True)
    z = _silu(cc * lax.rsqrt(var + EPS) * lng_ref[...] + lnb_ref[...]).astype(BF16)
    y = jnp.dot(z, w2_ref[...], preferred_element_type=F32) + b2_ref[...]
    o_ref[0] = x + mod[2:3, :] * y


def _conv(x, g, mod, w1, b1, wdw, bdw, lng, lnb, w2, b2, tm=512, row_chunk=64):
    b, s, d = x.shape
    tok = lambda i, j: (i, j, 0)
    vec = lambda n: pl.BlockSpec((1, n), lambda i, j: (0, 0))
    return pl.pallas_call(
        functools.partial(_conv_kernel, row_chunk=row_chunk),
        out_shape=jax.ShapeDtypeStruct(x.shape, F32),
        grid=(b, s // tm),
        in_specs=[
            pl.BlockSpec((1, tm, d), tok),
            vec(d),
            pl.BlockSpec((1, 3, d), lambda i, j: (i, 0, 0)),
            _resident((d, 2 * d), lambda i, j: (0, 0)),
            vec(2 * d),
            pl.BlockSpec((CONV_WIDTH, d), lambda i, j: (0, 0)),
            vec(d), vec(d), vec(d),
            _resident((d, d), lambda i, j: (0, 0)),
            vec(d),
        ],
        out_specs=pl.BlockSpec((1, tm, d), tok),
        scratch_shapes=[pltpu.VMEM((CONV_HALO + tm, d), F32), pltpu.VMEM((tm, d), F32)],
        compiler_params=_cparams(("parallel", "arbitrary")),
        name="conformer_conv",
    )(x, g.reshape(1, d), mod, w1.astype(BF16), b1.reshape(1, 2 * d), wdw, bdw.reshape(1, d),
      lng.reshape(1, d), lnb.reshape(1, d), w2.astype(BF16), b2.reshape(1, d))


def _bf16_pieces(x, n=3):
    out = []
    rest = np.float64(x)
    for _ in range(n):
        p = np.float64(np.asarray(rest, np.float32).astype(jnp.bfloat16).astype(np.float32))
        out.append(float(p))
        rest = rest - p
    return out


def _alibi_query_rows():
    ext = np.zeros((N_HEADS, 1, LANES), np.float32)
    for h in range(N_HEADS):
        slope = np.float32(2.0 ** (-8.0 * (h + 1) / N_HEADS))
        pieces = _bf16_pieces(float(slope) * LOG2E)
        ext[h, 0, HEAD_DIM:HEAD_DIM + 3] = pieces
        ext[h, 0, HEAD_DIM + 3:HEAD_DIM + 6] = [MOBA_BLOCK * p for p in pieces]
    return jnp.asarray(ext)


def _key_side_table(s):
    pos = np.arange(s)
    blk, off = pos // MOBA_BLOCK, pos % MOBA_BLOCK
    kx = np.zeros((s, LANES), np.float32)
    kx[pos, blk] = 1.0
    kx[:, HEAD_DIM:HEAD_DIM + 3] = off[:, None]
    kx[:, HEAD_DIM + 3:HEAD_DIM + 6] = blk[:, None]
    return jnp.asarray(kx, dtype=BF16)


def _head_indicator():
    e = np.zeros((D_MODEL, LANES), np.float32)
    e[np.arange(D_MODEL), np.arange(D_MODEL) // HEAD_DIM] = 1.0
    et = np.concatenate([e.T, e.T], axis=0)
    return jnp.asarray(e, dtype=BF16), jnp.asarray(et, dtype=BF16)


def _block_diag_kbar(kbar):
    b, nb = kbar.shape[0], kbar.shape[1]
    kb = kbar.reshape(b, nb, N_HEADS, HEAD_DIM).transpose(0, 2, 3, 1)
    kb = jnp.pad(kb, ((0, 0), (0, 0), (0, 0), (0, LANES - nb)))
    hpc = 2 * LANES // HEAD_DIM
    kb = kb.reshape(b, N_HEADS // hpc, hpc, HEAD_DIM, LANES)
    eye = jnp.eye(hpc, dtype=kb.dtype)
    out = jnp.einsum('bchdl,hg->bchdgl', kb, eye)
    return out.reshape(b, N_HEADS // hpc, hpc * HEAD_DIM, hpc * LANES).astype(BF16)


def _moba_mixer(x, g, mod, w_qkv, g_q, g_k, w_o):
    s = x.shape[1]
    assert s % MOBA_BLOCK == 0 and s // MOBA_BLOCK <= HEAD_DIM
    e_mat, et_mat = _head_indicator()
    q_scale = LOG2E / math.sqrt(HEAD_DIM)
    gq_row = (jnp.tile(g_q, N_HEADS) * q_scale).reshape(1, D_MODEL)
    gk_row = jnp.tile(g_k, N_HEADS).reshape(1, D_MODEL)
    q, k, v, kbar = _qkv(x, g, mod, w_qkv, gq_row, gk_row, e_mat, et_mat)
    selx = _gate(q, _block_diag_kbar(kbar), _alibi_query_rows())
    o = _attn(q, selx, k, v, _key_side_table(s))
    return _proj(o, x, mod, w_o)


def kernel(x, c, norm_g, ada_w, ada_b, ffn_w_in, ffn_w_out, attn_w_qkv, attn_g_q, attn_g_k, attn_w_o,
           conv_w_pw1, conv_b_pw1, conv_w_dw, conv_b_dw, conv_ln_g, conv_ln_b, conv_w_pw2, conv_b_pw2):
    depth = norm_g.shape[0]
    b = x.shape[0]
    mods = _ada(c, ada_w, ada_b).reshape(depth, b, N_SUB, 3, D_MODEL)
    for i in range(depth):
        mod = lambda j: mods[i, :, j]
        x = _ffn(x, norm_g[i, 0], mod(0), ffn_w_in[i, 0], ffn_w_out[i, 0])
        m = i // 2
        if i % 2 == 0:
            x = _moba_mixer(x, norm_g[i, 1], mod(1), attn_w_qkv[m], attn_g_q[m], attn_g_k[m], attn_w_o[m])
        else:
            x = _conv(x, norm_g[i, 1], mod(1), conv_w_pw1[m], conv_b_pw1[m], conv_w_dw[m], conv_b_dw[m],
                      conv_ln_g[m], conv_ln_b[m], conv_w_pw2[m], conv_b_pw2[m])
        x = _ffn(x, norm_g[i, 2], mod(2), ffn_w_in[i, 1], ffn_w_out[i, 1])
    return x
```

```python
import functools
import math

import numpy as np
import jax
import jax.numpy as jnp
from jax import lax
from jax.experimental import pallas as pl
from jax.experimental.pallas import tpu as pltpu

D_MODEL = 1024
N_HEADS = 16
HEAD_DIM = 64
MOBA_BLOCK = 256
MOBA_TOPK = 3
CONV_WIDTH = 31
D_FF = 2816
N_SUB = 3
EPS = 1e-6

LANES = 128
SUBLANES = 8
HEADS_PER_LANE_GROUP = LANES // HEAD_DIM
CONV_HALO = 32
MASK_BIAS = -1e30
LOG2E = math.log2(math.e)
VMEM_LIMIT = 56 * 1024 * 1024

F32 = jnp.float32
BF16 = jnp.bfloat16


def _cparams(sem):
    return pltpu.CompilerParams(dimension_semantics=sem, vmem_limit_bytes=VMEM_LIMIT)


def _resident(shape, index_map):
    return pl.BlockSpec(shape, index_map, pipeline_mode=pl.Buffered(1))


def _silu(x):
    return x * jax.nn.sigmoid(x)


def _modulated_norm(x, g, mod):
    ms = jnp.mean(x * x, axis=-1, keepdims=True)
    y = x * lax.rsqrt(ms + EPS) * g
    return y * (1.0 + mod[1:2, :]) + mod[0:1, :]


def _ada_kernel(ct_ref, w_ref, b_ref, o_ref, *, batch):
    ca = _silu(ct_ref[...])
    w = w_ref[0]
    for b in range(batch):
        o_ref[0, b:b + 1, :] = jnp.sum(w * ca[:, b:b + 1], axis=0, keepdims=True) + b_ref[0]


def _ada(c, ada_w, ada_b):
    depth, d, n = ada_w.shape
    batch = c.shape[0]
    tn = 1152
    return pl.pallas_call(
        functools.partial(_ada_kernel, batch=batch),
        out_shape=jax.ShapeDtypeStruct((depth, batch, n), F32),
        grid=(depth, n // tn),
        in_specs=[
            pl.BlockSpec((d, batch), lambda i, j: (0, 0)),
            pl.BlockSpec((1, d, tn), lambda i, j: (i, 0, j)),
            pl.BlockSpec((1, 1, tn), lambda i, j: (i, 0, j)),
        ],
        out_specs=pl.BlockSpec((1, batch, tn), lambda i, j: (i, 0, j)),
        compiler_params=_cparams(("arbitrary", "arbitrary")),
        name="adaln",
    )(c.T, ada_w, ada_b.reshape(depth, 1, n))


def _ffn_kernel(x_ref, g_ref, mod_ref, win_ref, wout_ref, o_ref):
    x = x_ref[0]
    mod = mod_ref[0]
    h = _modulated_norm(x, g_ref[...], mod).astype(BF16)
    u = jnp.dot(h, win_ref[...], preferred_element_type=F32)
    act = (_silu(u[:, :D_FF]) * u[:, D_FF:]).astype(BF16)
    y = jnp.dot(act, wout_ref[...], preferred_element_type=F32)
    o_ref[0] = x + (0.5 * mod[2:3, :]) * y


def _ffn(x, g, mod, w_in, w_out, tm=512):
    b, s, d = x.shape
    return pl.pallas_call(
        _ffn_kernel,
        out_shape=jax.ShapeDtypeStruct(x.shape, F32),
        grid=(b, s // tm),
        in_specs=[
            pl.BlockSpec((1, tm, d), lambda i, j: (i, j, 0)),
            pl.BlockSpec((1, d), lambda i, j: (0, 0)),
            pl.BlockSpec((1, 3, d), lambda i, j: (i, 0, 0)),
            _resident((d, 2 * D_FF), lambda i, j: (0, 0)),
            _resident((D_FF, d), lambda i, j: (0, 0)),
        ],
        out_specs=pl.BlockSpec((1, tm, d), lambda i, j: (i, j, 0)),
        compiler_params=_cparams(("parallel", "parallel")),
        name="ffn",
    )(x, g.reshape(1, d), mod, w_in.astype(BF16), w_out.astype(BF16))


def _head_rms_scale(t, e_ref, et_ref):
    ssq = jnp.dot((t * t).astype(BF16), e_ref[...], preferred_element_type=F32)
    r = lax.rsqrt(ssq * (1.0 / HEAD_DIM) + EPS)
    r_hi = r.astype(BF16)
    r_lo = (r - r_hi.astype(F32)).astype(BF16)
    return jnp.dot(jnp.concatenate([r_hi, r_lo], axis=1), et_ref[...], preferred_element_type=F32)


def _qkv_kernel(x_ref, g_ref, mod_ref, w_ref, gq_ref, gk_ref, e_ref, et_ref,
                q_ref, k_ref, v_ref, kbar_ref, *, blocks_per_tile):
    d = D_MODEL
    h = _modulated_norm(x_ref[0], g_ref[...], mod_ref[0]).astype(BF16)
    qkv = jnp.dot(h, w_ref[...], preferred_element_type=F32)
    q, k, v = qkv[:, :d], qkv[:, d:2 * d], qkv[:, 2 * d:]
    qn = q * _head_rms_scale(q, e_ref, et_ref) * gq_ref[...]
    kn = k * _head_rms_scale(k, e_ref, et_ref) * gk_ref[...]
    q_ref[0] = qn.astype(BF16)
    k_ref[0] = kn.astype(BF16)
    v_ref[0] = v.astype(BF16)
    for t in range(blocks_per_tile):
        blk = kn[t * MOBA_BLOCK:(t + 1) * MOBA_BLOCK, :]
        kbar_ref[0, t] = jnp.mean(blk, axis=0, keepdims=True)


def _qkv(x, g, mod, w_qkv, gq_row, gk_row, e_mat, et_mat, tm=512):
    b, s, d = x.shape
    nb = s // MOBA_BLOCK
    bpt = tm // MOBA_BLOCK
    tok = jax.ShapeDtypeStruct((b, s, d), BF16)
    tok_spec = pl.BlockSpec((1, tm, d), lambda i, j: (i, j, 0))
    return pl.pallas_call(
        functools.partial(_qkv_kernel, blocks_per_tile=bpt),
        out_shape=(tok, tok, tok, jax.ShapeDtypeStruct((b, nb, 1, d), F32)),
        grid=(b, s // tm),
        in_specs=[
            tok_spec,
            pl.BlockSpec((1, d), lambda i, j: (0, 0)),
            pl.BlockSpec((1, 3, d), lambda i, j: (i, 0, 0)),
            _resident((d, 3 * d), lambda i, j: (0, 0)),
            pl.BlockSpec((1, d), lambda i, j: (0, 0)),
            pl.BlockSpec((1, d), lambda i, j: (0, 0)),
            _resident((d, LANES), lambda i, j: (0, 0)),
            _resident((2 * LANES, d), lambda i, j: (0, 0)),
        ],
        out_specs=(tok_spec, tok_spec, tok_spec,
                   pl.BlockSpec((1, bpt, 1, d), lambda i, j: (i, j, 0, 0))),
        compiler_params=_cparams(("parallel", "parallel")),
        name="qkv",
    )(x, g.reshape(1, d), mod, w_qkv.astype(BF16), gq_row, gk_row, e_mat, et_mat)


def _gate_kernel(q_ref, kbar_ref, ext_ref, eye_ref, sel_ref):
    own = pl.program_id(1)
    nq = q_ref.shape[1]
    contract_last = (((1,), (1,)), ((), ()))
    blk = lax.broadcasted_iota(jnp.int32, (HEAD_DIM, nq), 0)
    blk_f = blk.astype(F32)
    past = blk < own
    lane = lax.broadcasted_iota(jnp.int32, (HEAD_DIM, LANES), 1)
    for p in range(D_MODEL // LANES):
        qp = q_ref[0, :, p * LANES:(p + 1) * LANES]
        kb = kbar_ref[0, :, p * LANES:(p + 1) * LANES].astype(BF16)
        per_head = [jnp.where((lane >= h * HEAD_DIM) & (lane < (h + 1) * HEAD_DIM), kb, jnp.zeros_like(kb))
                    for h in range(HEADS_PER_LANE_GROUP)]
        gates = lax.dot_general(jnp.concatenate(per_head, axis=0), qp, contract_last,
                                preferred_element_type=F32)
        parts = []
        for h in range(HEADS_PER_LANE_GROUP):
            g = jnp.where(past, gates[h * HEAD_DIM:(h + 1) * HEAD_DIM], -jnp.inf)
            taken = jnp.zeros(g.shape, jnp.bool_)
            for _ in range(MOBA_TOPK):
                gm = jnp.where(taken, -jnp.inf, g)
                m = jnp.max(gm, axis=0, keepdims=True)
                cand = jnp.logical_and(gm == m, jnp.logical_not(taken))
                idx = jnp.min(jnp.where(cand, blk_f, float(HEAD_DIM)), axis=0, keepdims=True)
                taken = jnp.logical_or(taken, blk_f == idx)
            chosen = jnp.logical_and(taken, past)
            parts.append(jnp.where(chosen, 0.0, MASK_BIAS).astype(BF16))
            parts.append(ext_ref[p * HEADS_PER_LANE_GROUP + h])
        out = lax.dot_general(eye_ref[...], jnp.concatenate(parts, axis=0), contract_last,
                              preferred_element_type=F32)
        width = HEADS_PER_LANE_GROUP * LANES
        sel_ref[0, :, p * width:(p + 1) * width] = out.astype(BF16)


def _gate(q, kbar, ext_t, eye):
    b, s, d = q.shape
    nb = s // MOBA_BLOCK
    return pl.pallas_call(
        _gate_kernel,
        out_shape=jax.ShapeDtypeStruct((b, s, N_HEADS * LANES), BF16),
        grid=(b, nb),
        in_specs=[
            pl.BlockSpec((1, MOBA_BLOCK, d), lambda i, j: (i, j, 0)),
            pl.BlockSpec((1, HEAD_DIM, d), lambda i, j: (i, 0, 0)),
            pl.BlockSpec((N_HEADS, HEAD_DIM, MOBA_BLOCK), lambda i, j: (0, 0, 0)),
            pl.BlockSpec((MOBA_BLOCK, MOBA_BLOCK), lambda i, j: (0, 0)),
        ],
        out_specs=pl.BlockSpec((1, MOBA_BLOCK, N_HEADS * LANES), lambda i, j: (i, j, 0)),
        compiler_params=_cparams(("parallel", "parallel")),
        name="moba_gate",
    )(q, kbar, ext_t, eye)


def _attn_kernel(q_ref, sel_ref, k_ref, v_ref, kx_ref, o_ref, qa_scr, qd_scr, m_scr, acc_scr,
                 *, group, kv_blocks):
    step = pl.program_id(2)
    tq = group * MOBA_BLOCK
    tkv = kv_blocks * MOBA_BLOCK
    chunks = group // kv_blocks
    heads = range(HEADS_PER_LANE_GROUP)
    rows = HEADS_PER_LANE_GROUP * tq
    lane = lax.broadcasted_iota(jnp.int32, (tq, LANES), 1)
    contract_last = (((1,), (1,)), ((), ()))

    def visit(chunk, lhs_scr, mask=None):
        start = pl.multiple_of(chunk * tkv, tkv)
        ka = jnp.concatenate([k_ref[0, pl.ds(start, tkv), :], kx_ref[pl.ds(start, tkv), :]], axis=1)
        s = lax.dot_general(lhs_scr[...], ka, contract_last, preferred_element_type=F32)
        if mask is not None:
            s = jnp.where(mask, s, MASK_BIAS)
        m_prev = m_scr[...]
        m_new = jnp.maximum(m_prev, jnp.max(s, axis=-1, keepdims=True))
        alpha = jnp.exp2(m_prev - m_new)
        p = jnp.exp2((s - jnp.concatenate([m_new] * (tkv // LANES), axis=1)).astype(BF16))
        vg = v_ref[0, pl.ds(start, tkv), :]
        kv_lane = lax.broadcasted_iota(jnp.int32, (tkv, LANES), 1)
        pv = []
        for h in heads:
            own_lanes = (kv_lane >= h * HEAD_DIM) & (kv_lane < (h + 1) * HEAD_DIM)
            va = jnp.where(own_lanes, vg, jnp.ones_like(vg))
            pv.append(jnp.dot(p[h * tq:(h + 1) * tq], va, preferred_element_type=F32))
        acc_scr[...] = alpha * acc_scr[...] + jnp.concatenate(pv, axis=0)
        m_scr[...] = m_new

    qp = q_ref[0]
    row_block = lax.shift_right_logical(lax.broadcasted_iota(jnp.int32, (tq, LANES), 0),
                                        MOBA_BLOCK.bit_length() - 1) + step * group
    for h in heads:
        in_head = (lane >= h * HEAD_DIM) & (lane < (h + 1) * HEAD_DIM)
        qh = jnp.where(in_head, qp, jnp.zeros_like(qp))
        sx = sel_ref[0, :, h * LANES:(h + 1) * LANES]
        qa_scr[h * tq:(h + 1) * tq, :] = jnp.concatenate([qh, sx], axis=1)
        qd_scr[h * tq:(h + 1) * tq, :] = jnp.concatenate(
            [qh, jnp.where(lane == row_block, jnp.zeros_like(sx), sx)], axis=1)
    m_scr[...] = jnp.full((rows, LANES), MASK_BIAS, F32)
    acc_scr[...] = jnp.zeros((rows, LANES), F32)

    query = lax.rem(lax.broadcasted_iota(jnp.int32, (rows, tkv), 0), tq)
    col = lax.broadcasted_iota(jnp.int32, (rows, tkv), 1)
    for u in range(chunks):
        visit(step * chunks + u, qd_scr, query >= col + u * tkv)

    def body(it, carry):
        for u in range(chunks):
            visit(it * chunks + u, qa_scr)
        return carry

    lax.fori_loop(0, step, body, 0)
    acc = acc_scr[...]
    out = acc / pltpu.roll(acc, HEAD_DIM, axis=1)
    o_ref[0] = jnp.where(lane < HEAD_DIM, out[:tq], out[tq:]).astype(BF16)


def _attn(q, selx, k, v, kx, group=4, kv_blocks=2):
    b, s, d = q.shape
    pairs = d // LANES
    tq = group * MOBA_BLOCK
    return pl.pallas_call(
        functools.partial(_attn_kernel, group=group, kv_blocks=kv_blocks),
        out_shape=jax.ShapeDtypeStruct((b, s, d), BF16),
        grid=(b, pairs, s // tq),
        in_specs=[
            pl.BlockSpec((1, tq, LANES), lambda bi, p, i: (bi, i, p)),
            pl.BlockSpec((1, tq, HEADS_PER_LANE_GROUP * LANES), lambda bi, p, i: (bi, i, p)),
            pl.BlockSpec((1, s, LANES), lambda bi, p, i: (bi, 0, p)),
            pl.BlockSpec((1, s, LANES), lambda bi, p, i: (bi, 0, p)),
            _resident((s, LANES), lambda bi, p, i: (0, 0)),
        ],
        out_specs=pl.BlockSpec((1, tq, LANES), lambda bi, p, i: (bi, i, p)),
        scratch_shapes=[
            pltpu.VMEM((HEADS_PER_LANE_GROUP * tq, 2 * LANES), BF16),
            pltpu.VMEM((HEADS_PER_LANE_GROUP * tq, 2 * LANES), BF16),
            pltpu.VMEM((HEADS_PER_LANE_GROUP * tq, LANES), F32),
            pltpu.VMEM((HEADS_PER_LANE_GROUP * tq, LANES), F32),
        ],
        compiler_params=_cparams(("parallel", "parallel", "arbitrary")),
        name="moba_attn",
    )(q, selx, k, v, kx)


def _proj_kernel(o_ref, x_ref, mod_ref, w_ref, y_ref):
    y = jnp.dot(o_ref[0], w_ref[...], preferred_element_type=F32)
    y_ref[0] = x_ref[0] + mod_ref[0][2:3, :] * y


def _proj(o, x, mod, w, tm=512):
    b, s, d = x.shape
    tok = lambda i, j: (i, j, 0)
    return pl.pallas_call(
        _proj_kernel,
        out_shape=jax.ShapeDtypeStruct(x.shape, F32),
        grid=(b, s // tm),
        in_specs=[
            pl.BlockSpec((1, tm, d), tok),
            pl.BlockSpec((1, tm, d), tok),
            pl.BlockSpec((1, 3, d), lambda i, j: (i, 0, 0)),
            _resident((d, d), lambda i, j: (0, 0)),
        ],
        out_specs=pl.BlockSpec((1, tm, d), tok),
        compiler_params=_cparams(("parallel", "parallel")),
        name="attn_out_proj",
    )(o, x, mod, w.astype(BF16))


def _conv_kernel(x_ref, g_ref, mod_ref, w1_ref, b1_ref, wdw_ref, bdw_ref, lng_ref, lnb_ref,
                 w2_ref, b2_ref, o_ref, u_scr, c_scr, *, row_chunk):
    d = D_MODEL
    tm = x_ref.shape[1]
    x = x_ref[0]
    mod = mod_ref[0]

    @pl.when(pl.program_id(1) == 0)
    def _():
        u_scr[0:CONV_HALO, :] = jnp.zeros((CONV_HALO, d), F32)

    h = _modulated_norm(x, g_ref[...], mod).astype(BF16)
    u = jnp.dot(h, w1_ref[...], preferred_element_type=F32) + b1_ref[...]
    u_scr[CONV_HALO:CONV_HALO + tm, :] = u[:, :d] * jax.nn.sigmoid(u[:, d:])

    base = CONV_HALO - (CONV_WIDTH - 1)

    def chunk(r, carry):
        r0 = pl.multiple_of(r * row_chunk, row_chunk)
        for lc in range(d // LANES):
            ls = slice(lc * LANES, (lc + 1) * LANES)
            win = u_scr[pl.ds(r0, row_chunk + CONV_HALO), ls]
            acc = jnp.broadcast_to(bdw_ref[:, ls], (row_chunk, LANES))
            for shift in range(SUBLANES):
                taps = [w for w in range(CONV_WIDTH) if (base + w) % SUBLANES == shift]
                shifted = win if shift == 0 else pltpu.roll(win, win.shape[0] - shift, axis=0)
                for w in taps:
                    a0 = (base + w) // SUBLANES * SUBLANES
                    acc = acc + shifted[a0:a0 + row_chunk] * wdw_ref[w:w + 1, ls]
            c_scr[pl.ds(r0, row_chunk), ls] = acc
        return carry

    lax.fori_loop(0, tm // row_chunk, chunk, 0)
    u_scr[0:CONV_HALO, :] = u_scr[tm:tm + CONV_HALO, :]

    c = c_scr[...]
    mu = jnp.mean(c, axis=-1, keepdims=True)
    cc = c - mu
    var = jnp.mean(cc * cc, axis=-1, keepdims=True)
    z = _silu(cc * lax.rsqrt(var + EPS) * lng_ref[...] + lnb_ref[...]).astype(BF16)
    y = jnp.dot(z, w2_ref[...], preferred_element_type=F32) + b2_ref[...]
    o_ref[0] = x + mod[2:3, :] * y


def _conv(x, g, mod, w1, b1, wdw, bdw, lng, lnb, w2, b2, tm=512, row_chunk=64):
    b, s, d = x.shape
    tok = lambda i, j: (i, j, 0)
    vec = lambda n: pl.BlockSpec((1, n), lambda i, j: (0, 0))
    return pl.pallas_call(
        functools.partial(_conv_kernel, row_chunk=row_chunk),
        out_shape=jax.ShapeDtypeStruct(x.shape, F32),
        grid=(b, s // tm),
        in_specs=[
            pl.BlockSpec((1, tm, d), tok),
            vec(d),
            pl.BlockSpec((1, 3, d), lambda i, j: (i, 0, 0)),
            _resident((d, 2 * d), lambda i, j: (0, 0)),
            vec(2 * d),
            pl.BlockSpec((CONV_WIDTH, d), lambda i, j: (0, 0)),
            vec(d), vec(d), vec(d),
            _resident((d, d), lambda i, j: (0, 0)),
            vec(d),
        ],
        out_specs=pl.BlockSpec((1, tm, d), tok),
        scratch_shapes=[pltpu.VMEM((CONV_HALO + tm, d), F32), pltpu.VMEM((tm, d), F32)],
        compiler_params=_cparams(("parallel", "arbitrary")),
        name="conformer_conv",
    )(x, g.reshape(1, d), mod, w1.astype(BF16), b1.reshape(1, 2 * d), wdw, bdw.reshape(1, d),
      lng.reshape(1, d), lnb.reshape(1, d), w2.astype(BF16), b2.reshape(1, d))


def _bf16_pieces(x, n=3):
    out = []
    rest = np.float64(x)
    for _ in range(n):
        p = np.float64(np.asarray(rest, np.float32).astype(jnp.bfloat16).astype(np.float32))
        out.append(float(p))
        rest = rest - p
    return out


def _alibi_query_rows():
    ext = np.zeros((N_HEADS, HEAD_DIM, MOBA_BLOCK), np.float32)
    for h in range(N_HEADS):
        slope = np.float32(2.0 ** (-8.0 * (h + 1) / N_HEADS))
        pieces = _bf16_pieces(float(slope) * LOG2E)
        for e, piece in enumerate(pieces + [MOBA_BLOCK * p for p in pieces]):
            ext[h, e, :] = piece
    return jnp.asarray(ext, dtype=BF16)


def _key_side_table(s):
    pos = np.arange(s)
    blk, off = pos // MOBA_BLOCK, pos % MOBA_BLOCK
    kx = np.zeros((s, LANES), np.float32)
    kx[pos, blk] = 1.0
    kx[:, HEAD_DIM:HEAD_DIM + 3] = off[:, None]
    kx[:, HEAD_DIM + 3:HEAD_DIM + 6] = blk[:, None]
    return jnp.asarray(kx, dtype=BF16)


def _head_indicator():
    e = np.zeros((D_MODEL, LANES), np.float32)
    e[np.arange(D_MODEL), np.arange(D_MODEL) // HEAD_DIM] = 1.0
    et = np.concatenate([e.T, e.T], axis=0)
    return jnp.asarray(e, dtype=BF16), jnp.asarray(et, dtype=BF16)


def _moba_mixer(x, g, mod, w_qkv, g_q, g_k, w_o):
    s = x.shape[1]
    assert s % MOBA_BLOCK == 0 and s // MOBA_BLOCK <= HEAD_DIM
    e_mat, et_mat = _head_indicator()
    q_scale = LOG2E / math.sqrt(HEAD_DIM)
    gq_row = (jnp.tile(g_q, N_HEADS) * q_scale).reshape(1, D_MODEL)
    gk_row = jnp.tile(g_k, N_HEADS).reshape(1, D_MODEL)
    q, k, v, kbar = _qkv(x, g, mod, w_qkv, gq_row, gk_row, e_mat, et_mat)
    nb = s // MOBA_BLOCK
    kbar = jnp.pad(kbar.reshape(-1, nb, D_MODEL), ((0, 0), (0, HEAD_DIM - nb), (0, 0)))
    selx = _gate(q, kbar, _alibi_query_rows(), jnp.eye(MOBA_BLOCK, dtype=BF16))
    o = _attn(q, selx, k, v, _key_side_table(s))
    return _proj(o, x, mod, w_o)


def kernel(x, c, norm_g, ada_w, ada_b, ffn_w_in, ffn_w_out, attn_w_qkv, attn_g_q, attn_g_k, attn_w_o,
           conv_w_pw1, conv_b_pw1, conv_w_dw, conv_b_dw, conv_ln_g, conv_ln_b, conv_w_pw2, conv_b_pw2):
    depth = norm_g.shape[0]
    b = x.shape[0]
    mods = _ada(c, ada_w, ada_b).reshape(depth, b, N_SUB, 3, D_MODEL)
    for i in range(depth):
        mod = lambda j: mods[i, :, j]
        x = _ffn(x, norm_g[i, 0], mod(0), ffn_w_in[i, 0], ffn_w_out[i, 0])
        m = i // 2
        if i % 2 == 0:
            x = _moba_mixer(x, norm_g[i, 1], mod(1), attn_w_qkv[m], attn_g_q[m], attn_g_k[m], attn_w_o[m])
        else:
            x = _conv(x, norm_g[i, 1], mod(1), conv_w_pw1[m], conv_b_pw1[m], conv_w_dw[m], conv_b_dw[m],
                      conv_ln_g[m], conv_ln_b[m], conv_w_pw2[m], conv_b_pw2[m])
        x = _ffn(x, norm_g[i, 2], mod(2), ffn_w_in[i, 1], ffn_w_out[i, 1])
    return x
```

```python
import functools
import math

import numpy as np
import jax
import jax.numpy as jnp
from jax import lax
from jax.experimental import pallas as pl
from jax.experimental.pallas import tpu as pltpu

D_MODEL = 1024
N_HEADS = 16
HEAD_DIM = 64
MOBA_BLOCK = 256
MOBA_TOPK = 3
CONV_WIDTH = 31
D_FF = 2816
N_SUB = 3
EPS = 1e-6

LANES = 128
SUBLANES = 8
HEADS_PER_LANE_GROUP = LANES // HEAD_DIM
CONV_HALO = 32
MASK_BIAS = -1e30
LOG2E = math.log2(math.e)
VMEM_LIMIT = 56 * 1024 * 1024

F32 = jnp.float32
BF16 = jnp.bfloat16


def _cparams(sem):
    return pltpu.CompilerParams(dimension_semantics=sem, vmem_limit_bytes=VMEM_LIMIT)


def _resident(shape, index_map):
    return pl.BlockSpec(shape, index_map, pipeline_mode=pl.Buffered(1))


def _silu(x):
    return x * jax.nn.sigmoid(x)


def _modulated_norm(x, g, mod):
    ms = jnp.mean(x * x, axis=-1, keepdims=True)
    y = x * lax.rsqrt(ms + EPS) * g
    return y * (1.0 + mod[1:2, :]) + mod[0:1, :]


def _ada_kernel(ct_ref, w_ref, b_ref, o_ref, *, batch):
    ca = _silu(ct_ref[...])
    w = w_ref[0]
    for b in range(batch):
        o_ref[0, b:b + 1, :] = jnp.sum(w * ca[:, b:b + 1], axis=0, keepdims=True) + b_ref[0]


def _ada(c, ada_w, ada_b):
    depth, d, n = ada_w.shape
    batch = c.shape[0]
    tn = 1152
    return pl.pallas_call(
        functools.partial(_ada_kernel, batch=batch),
        out_shape=jax.ShapeDtypeStruct((depth, batch, n), F32),
        grid=(depth, n // tn),
        in_specs=[
            pl.BlockSpec((d, batch), lambda i, j: (0, 0)),
            pl.BlockSpec((1, d, tn), lambda i, j: (i, 0, j)),
            pl.BlockSpec((1, 1, tn), lambda i, j: (i, 0, j)),
        ],
        out_specs=pl.BlockSpec((1, batch, tn), lambda i, j: (i, 0, j)),
        compiler_params=_cparams(("arbitrary", "arbitrary")),
        name="adaln",
    )(c.T, ada_w, ada_b.reshape(depth, 1, n))


def _ffn_kernel(x_ref, g_ref, mod_ref, win_ref, wout_ref, o_ref):
    x = x_ref[0]
    mod = mod_ref[0]
    h = _modulated_norm(x, g_ref[...], mod).astype(BF16)
    u = jnp.dot(h, win_ref[...], preferred_element_type=F32)
    act = (_silu(u[:, :D_FF]) * u[:, D_FF:]).astype(BF16)
    y = jnp.dot(act, wout_ref[...], preferred_element_type=F32)
    o_ref[0] = x + (0.5 * mod[2:3, :]) * y


def _ffn(x, g, mod, w_in, w_out, tm=512):
    b, s, d = x.shape
    return pl.pallas_call(
        _ffn_kernel,
        out_shape=jax.ShapeDtypeStruct(x.shape, F32),
        grid=(b, s // tm),
        in_specs=[
            pl.BlockSpec((1, tm, d), lambda i, j: (i, j, 0)),
            pl.BlockSpec((1, d), lambda i, j: (0, 0)),
            pl.BlockSpec((1, 3, d), lambda i, j: (i, 0, 0)),
            _resident((d, 2 * D_FF), lambda i, j: (0, 0)),
            _resident((D_FF, d), lambda i, j: (0, 0)),
        ],
        out_specs=pl.BlockSpec((1, tm, d), lambda i, j: (i, j, 0)),
        compiler_params=_cparams(("parallel", "parallel")),
        name="ffn",
    )(x, g.reshape(1, d), mod, w_in.astype(BF16), w_out.astype(BF16))


def _head_rms_scale(t, e_ref, et_ref):
    ssq = jnp.dot((t * t).astype(BF16), e_ref[...], preferred_element_type=F32)
    r = lax.rsqrt(ssq * (1.0 / HEAD_DIM) + EPS)
    r_hi = r.astype(BF16)
    r_lo = (r - r_hi.astype(F32)).astype(BF16)
    return jnp.dot(jnp.concatenate([r_hi, r_lo], axis=1), et_ref[...], preferred_element_type=F32)


def _qkv_kernel(x_ref, g_ref, mod_ref, w_ref, gq_ref, gk_ref, e_ref, et_ref,
                q_ref, k_ref, v_ref, kbar_ref, *, blocks_per_tile):
    d = D_MODEL
    h = _modulated_norm(x_ref[0], g_ref[...], mod_ref[0]).astype(BF16)
    qkv = jnp.dot(h, w_ref[...], preferred_element_type=F32)
    q, k, v = qkv[:, :d], qkv[:, d:2 * d], qkv[:, 2 * d:]
    qn = q * _head_rms_scale(q, e_ref, et_ref) * gq_ref[...]
    kn = k * _head_rms_scale(k, e_ref, et_ref) * gk_ref[...]
    q_ref[0] = qn.astype(BF16)
    k_ref[0] = kn.astype(BF16)
    v_ref[0] = v.astype(BF16)
    for t in range(blocks_per_tile):
        blk = kn[t * MOBA_BLOCK:(t + 1) * MOBA_BLOCK, :]
        kbar_ref[0, t] = jnp.mean(blk, axis=0, keepdims=True)


def _qkv(x, g, mod, w_qkv, gq_row, gk_row, e_mat, et_mat, tm=512):
    b, s, d = x.shape
    nb = s // MOBA_BLOCK
    bpt = tm // MOBA_BLOCK
    tok = jax.ShapeDtypeStruct((b, s, d), BF16)
    tok_spec = pl.BlockSpec((1, tm, d), lambda i, j: (i, j, 0))
    return pl.pallas_call(
        functools.partial(_qkv_kernel, blocks_per_tile=bpt),
        out_shape=(tok, tok, tok, jax.ShapeDtypeStruct((b, nb, 1, d), F32)),
        grid=(b, s // tm),
        in_specs=[
            tok_spec,
            pl.BlockSpec((1, d), lambda i, j: (0, 0)),
            pl.BlockSpec((1, 3, d), lambda i, j: (i, 0, 0)),
            _resident((d, 3 * d), lambda i, j: (0, 0)),
            pl.BlockSpec((1, d), lambda i, j: (0, 0)),
            pl.BlockSpec((1, d), lambda i, j: (0, 0)),
            _resident((d, LANES), lambda i, j: (0, 0)),
            _resident((2 * LANES, d), lambda i, j: (0, 0)),
        ],
        out_specs=(tok_spec, tok_spec, tok_spec,
                   pl.BlockSpec((1, bpt, 1, d), lambda i, j: (i, j, 0, 0))),
        compiler_params=_cparams(("parallel", "parallel")),
        name="qkv",
    )(x, g.reshape(1, d), mod, w_qkv.astype(BF16), gq_row, gk_row, e_mat, et_mat)


def _gate_kernel(q_ref, kbar_ref, ext_ref, eye_ref, sel_ref):
    own = pl.program_id(1)
    nq = q_ref.shape[1]
    contract_last = (((1,), (1,)), ((), ()))
    blk = lax.broadcasted_iota(jnp.int32, (HEAD_DIM, nq), 0)
    blk_f = blk.astype(F32)
    past = blk < own
    lane = lax.broadcasted_iota(jnp.int32, (HEAD_DIM, LANES), 1)
    for p in range(D_MODEL // LANES):
        qp = q_ref[0, :, p * LANES:(p + 1) * LANES]
        kb = kbar_ref[0, :, p * LANES:(p + 1) * LANES].astype(BF16)
        per_head = [jnp.where((lane >= h * HEAD_DIM) & (lane < (h + 1) * HEAD_DIM), kb, jnp.zeros_like(kb))
                    for h in range(HEADS_PER_LANE_GROUP)]
        gates = lax.dot_general(jnp.concatenate(per_head, axis=0), qp, contract_last,
                                preferred_element_type=F32)
        parts = []
        for h in range(HEADS_PER_LANE_GROUP):
            g = jnp.where(past, gates[h * HEAD_DIM:(h + 1) * HEAD_DIM], -jnp.inf)
            taken = jnp.zeros(g.shape, jnp.bool_)
            for _ in range(MOBA_TOPK):
                gm = jnp.where(taken, -jnp.inf, g)
                m = jnp.max(gm, axis=0, keepdims=True)
                cand = jnp.logical_and(gm == m, jnp.logical_not(taken))
                idx = jnp.min(jnp.where(cand, blk_f, float(HEAD_DIM)), axis=0, keepdims=True)
                taken = jnp.logical_or(taken, blk_f == idx)
            chosen = jnp.logical_and(taken, past)
            parts.append(jnp.where(chosen, 0.0, MASK_BIAS).astype(BF16))
            parts.append(ext_ref[p * HEADS_PER_LANE_GROUP + h])
        out = lax.dot_general(eye_ref[...], jnp.concatenate(parts, axis=0), contract_last,
                              preferred_element_type=F32)
        width = HEADS_PER_LANE_GROUP * LANES
        sel_ref[0, :, p * width:(p + 1) * width] = out.astype(BF16)


def _gate(q, kbar, ext_t, eye):
    b, s, d = q.shape
    nb = s // MOBA_BLOCK
    return pl.pallas_call(
        _gate_kernel,
        out_shape=jax.ShapeDtypeStruct((b, s, N_HEADS * LANES), BF16),
        grid=(b, nb),
        in_specs=[
            pl.BlockSpec((1, MOBA_BLOCK, d), lambda i, j: (i, j, 0)),
            pl.BlockSpec((1, HEAD_DIM, d), lambda i, j: (i, 0, 0)),
            pl.BlockSpec((N_HEADS, HEAD_DIM, MOBA_BLOCK), lambda i, j: (0, 0, 0)),
            pl.BlockSpec((MOBA_BLOCK, MOBA_BLOCK), lambda i, j: (0, 0)),
        ],
        out_specs=pl.BlockSpec((1, MOBA_BLOCK, N_HEADS * LANES), lambda i, j: (i, j, 0)),
        compiler_params=_cparams(("parallel", "parallel")),
        name="moba_gate",
    )(q, kbar, ext_t, eye)


def _attn_kernel(q_ref, sel_ref, k_ref, v_ref, kx_ref, o_ref, qa_scr, qd_scr, m_scr, acc_scr,
                 *, group, kv_blocks):
    step = pl.program_id(2)
    tq = group * MOBA_BLOCK
    tkv = kv_blocks * MOBA_BLOCK
    chunks = group // kv_blocks
    heads = range(HEADS_PER_LANE_GROUP)
    rows = HEADS_PER_LANE_GROUP * tq
    lane = lax.broadcasted_iota(jnp.int32, (tq, LANES), 1)
    contract_last = (((1,), (1,)), ((), ()))

    def visit(chunk, lhs_scr, first_query=0):
        start = pl.multiple_of(chunk * tkv, tkv)
        ka = jnp.concatenate([k_ref[0, pl.ds(start, tkv), :], kx_ref[pl.ds(start, tkv), :]], axis=1)
        vg = v_ref[0, pl.ds(start, tkv), :]
        kv_lane = lax.broadcasted_iota(jnp.int32, (tkv, LANES), 1)
        va = [jnp.where((kv_lane >= h * HEAD_DIM) & (kv_lane < (h + 1) * HEAD_DIM), vg, jnp.ones_like(vg))
              for h in heads]
        if first_query == 0:
            spans = [(0, rows)]
        else:
            spans = [(h * tq + first_query, (h + 1) * tq) for h in heads]
        for lo, hi in spans:
            s = lax.dot_general(lhs_scr[lo:hi], ka, contract_last, preferred_element_type=F32)
            if lhs_scr is qd_scr:
                query = lax.rem(lax.broadcasted_iota(jnp.int32, s.shape, 0) + lo, tq)
                s = jnp.where(query >= lax.broadcasted_iota(jnp.int32, s.shape, 1) + first_query, s, MASK_BIAS)
            m_prev = m_scr[lo:hi]
            m_new = jnp.maximum(m_prev, jnp.max(s, axis=-1, keepdims=True))
            alpha = jnp.exp2(m_prev - m_new)
            p = jnp.exp2((s - jnp.concatenate([m_new] * (tkv // LANES), axis=1)).astype(BF16))
            pv = [jnp.dot(p[max(lo, h * tq) - lo:min(hi, (h + 1) * tq) - lo], va[h], preferred_element_type=F32)
                  for h in heads if max(lo, h * tq) < min(hi, (h + 1) * tq)]
            acc_scr[lo:hi] = alpha * acc_scr[lo:hi] + jnp.concatenate(pv, axis=0)
            m_scr[lo:hi] = m_new

    qp = q_ref[0]
    row_block = lax.shift_right_logical(lax.broadcasted_iota(jnp.int32, (tq, LANES), 0),
                                        MOBA_BLOCK.bit_length() - 1) + step * group
    for h in heads:
        in_head = (lane >= h * HEAD_DIM) & (lane < (h + 1) * HEAD_DIM)
        qh = jnp.where(in_head, qp, jnp.zeros_like(qp))
        sx = sel_ref[0, :, h * LANES:(h + 1) * LANES]
        qa_scr[h * tq:(h + 1) * tq, :] = jnp.concatenate([qh, sx], axis=1)
        qd_scr[h * tq:(h + 1) * tq, :] = jnp.concatenate(
            [qh, jnp.where(lane == row_block, jnp.zeros_like(sx), sx)], axis=1)
    m_scr[...] = jnp.full((rows, LANES), MASK_BIAS, F32)
    acc_scr[...] = jnp.zeros((rows, LANES), F32)

    for u in range(chunks):
        visit(step * chunks + u, qd_scr, first_query=u * tkv)

    def body(it, carry):
        for u in range(chunks):
            visit(it * chunks + u, qa_scr)
        return carry

    lax.fori_loop(0, step, body, 0)
    acc = acc_scr[...]
    out = acc / pltpu.roll(acc, HEAD_DIM, axis=1)
    o_ref[0] = jnp.where(lane < HEAD_DIM, out[:tq], out[tq:]).astype(BF16)


def _attn(q, selx, k, v, kx, group=8, kv_blocks=2):
    b, s, d = q.shape
    pairs = d // LANES
    tq = group * MOBA_BLOCK
    return pl.pallas_call(
        functools.partial(_attn_kernel, group=group, kv_blocks=kv_blocks),
        out_shape=jax.ShapeDtypeStruct((b, s, d), BF16),
        grid=(b, pairs, s // tq),
        in_specs=[
            pl.BlockSpec((1, tq, LANES), lambda bi, p, i: (bi, i, p)),
            pl.BlockSpec((1, tq, HEADS_PER_LANE_GROUP * LANES), lambda bi, p, i: (bi, i, p)),
            pl.BlockSpec((1, s, LANES), lambda bi, p, i: (bi, 0, p)),
            pl.BlockSpec((1, s, LANES), lambda bi, p, i: (bi, 0, p)),
            _resident((s, LANES), lambda bi, p, i: (0, 0)),
        ],
        out_specs=pl.BlockSpec((1, tq, LANES), lambda bi, p, i: (bi, i, p)),
        scratch_shapes=[
            pltpu.VMEM((HEADS_PER_LANE_GROUP * tq, 2 * LANES), BF16),
            pltpu.VMEM((HEADS_PER_LANE_GROUP * tq, 2 * LANES), BF16),
            pltpu.VMEM((HEADS_PER_LANE_GROUP * tq, LANES), F32),
            pltpu.VMEM((HEADS_PER_LANE_GROUP * tq, LANES), F32),
        ],
        compiler_params=_cparams(("parallel", "parallel", "arbitrary")),
        name="moba_attn",
    )(q, selx, k, v, kx)


def _proj_kernel(o_ref, x_ref, mod_ref, w_ref, y_ref):
    y = jnp.dot(o_ref[0], w_ref[...], preferred_element_type=F32)
    y_ref[0] = x_ref[0] + mod_ref[0][2:3, :] * y


def _proj(o, x, mod, w, tm=512):
    b, s, d = x.shape
    tok = lambda i, j: (i, j, 0)
    return pl.pallas_call(
        _proj_kernel,
        out_shape=jax.ShapeDtypeStruct(x.shape, F32),
        grid=(b, s // tm),
        in_specs=[
            pl.BlockSpec((1, tm, d), tok),
            pl.BlockSpec((1, tm, d), tok),
            pl.BlockSpec((1, 3, d), lambda i, j: (i, 0, 0)),
            _resident((d, d), lambda i, j: (0, 0)),
        ],
        out_specs=pl.BlockSpec((1, tm, d), tok),
        compiler_params=_cparams(("parallel", "parallel")),
        name="attn_out_proj",
    )(o, x, mod, w.astype(BF16))


def _conv_kernel(x_ref, g_ref, mod_ref, w1_ref, b1_ref, wdw_ref, bdw_ref, lng_ref, lnb_ref,
                 w2_ref, b2_ref, o_ref, u_scr, c_scr, *, row_chunk):
    d = D_MODEL
    tm = x_ref.shape[1]
    x = x_ref[0]
    mod = mod_ref[0]

    @pl.when(pl.program_id(1) == 0)
    def _():
        u_scr[0:CONV_HALO, :] = jnp.zeros((CONV_HALO, d), F32)

    h = _modulated_norm(x, g_ref[...], mod).astype(BF16)
    u = jnp.dot(h, w1_ref[...], preferred_element_type=F32) + b1_ref[...]
    u_scr[CONV_HALO:CONV_HALO + tm, :] = u[:, :d] * jax.nn.sigmoid(u[:, d:])

    base = CONV_HALO - (CONV_WIDTH - 1)

    def chunk(r, carry):
        r0 = pl.multiple_of(r * row_chunk, row_chunk)
        for lc in range(d // LANES):
            ls = slice(lc * LANES, (lc + 1) * LANES)
            win = u_scr[pl.ds(r0, row_chunk + CONV_HALO), ls]
            acc = jnp.broadcast_to(bdw_ref[:, ls], (row_chunk, LANES))
            for shift in range(SUBLANES):
                taps = [w for w in range(CONV_WIDTH) if (base + w) % SUBLANES == shift]
                shifted = win if shift == 0 else pltpu.roll(win, win.shape[0] - shift, axis=0)
                for w in taps:
                    a0 = (base + w) // SUBLANES * SUBLANES
                    acc = acc + shifted[a0:a0 + row_chunk] * wdw_ref[w:w + 1, ls]
            c_scr[pl.ds(r0, row_chunk), ls] = acc
        return carry

    lax.fori_loop(0, tm // row_chunk, chunk, 0)
    u_scr[0:CONV_HALO, :] = u_scr[tm:tm + CONV_HALO, :]

    c = c_scr[...]
    mu = jnp.mean(c, axis=-1, keepdims=True)
    cc = c - mu
    var = jnp.mean(cc * cc, axis=-1, keepdims=True)
    z = _silu(cc * lax.rsqrt(var + EPS) * lng_ref[...] + lnb_ref[...]).astype(BF16)
    y = jnp.dot(z, w2_ref[...], preferred_element_type=F32) + b2_ref[...]
    o_ref[0] = x + mod[2:3, :] * y


def _conv(x, g, mod, w1, b1, wdw, bdw, lng, lnb, w2, b2, tm=512, row_chunk=64):
    b, s, d = x.shape
    tok = lambda i, j: (i, j, 0)
    vec = lambda n: pl.BlockSpec((1, n), lambda i, j: (0, 0))
    return pl.pallas_call(
        functools.partial(_conv_kernel, row_chunk=row_chunk),
        out_shape=jax.ShapeDtypeStruct(x.shape, F32),
        grid=(b, s // tm),
        in_specs=[
            pl.BlockSpec((1, tm, d), tok),
            vec(d),
            pl.BlockSpec((1, 3, d), lambda i, j: (i, 0, 0)),
            _resident((d, 2 * d), lambda i, j: (0, 0)),
            vec(2 * d),
            pl.BlockSpec((CONV_WIDTH, d), lambda i, j: (0, 0)),
            vec(d), vec(d), vec(d),
            _resident((d, d), lambda i, j: (0, 0)),
            vec(d),
        ],
        out_specs=pl.BlockSpec((1, tm, d), tok),
        scratch_shapes=[pltpu.VMEM((CONV_HALO + tm, d), F32), pltpu.VMEM((tm, d), F32)],
        compiler_params=_cparams(("parallel", "arbitrary")),
        name="conformer_conv",
    )(x, g.reshape(1, d), mod, w1.astype(BF16), b1.reshape(1, 2 * d), wdw, bdw.reshape(1, d),
      lng.reshape(1, d), lnb.reshape(1, d), w2.astype(BF16), b2.reshape(1, d))


def _bf16_pieces(x, n=3):
    out = []
    rest = np.float64(x)
    for _ in range(n):
        p = np.float64(np.asarray(rest, np.float32).astype(jnp.bfloat16).astype(np.float32))
        out.append(float(p))
        rest = rest - p
    return out


def _alibi_query_rows():
    ext = np.zeros((N_HEADS, HEAD_DIM, MOBA_BLOCK), np.float32)
    for h in range(N_HEADS):
        slope = np.float32(2.0 ** (-8.0 * (h + 1) / N_HEADS))
        pieces = _bf16_pieces(float(slope) * LOG2E)
        for e, piece in enumerate(pieces + [MOBA_BLOCK * p for p in pieces]):
            ext[h, e, :] = piece
    return jnp.asarray(ext, dtype=BF16)


def _key_side_table(s):
    pos = np.arange(s)
    blk, off = pos // MOBA_BLOCK, pos % MOBA_BLOCK
    kx = np.zeros((s, LANES), np.float32)
    kx[pos, blk] = 1.0
    kx[:, HEAD_DIM:HEAD_DIM + 3] = off[:, None]
    kx[:, HEAD_DIM + 3:HEAD_DIM + 6] = blk[:, None]
    return jnp.asarray(kx, dtype=BF16)


def _head_indicator():
    e = np.zeros((D_MODEL, LANES), np.float32)
    e[np.arange(D_MODEL), np.arange(D_MODEL) // HEAD_DIM] = 1.0
    et = np.concatenate([e.T, e.T], axis=0)
    return jnp.asarray(e, dtype=BF16), jnp.asarray(et, dtype=BF16)


def _moba_mixer(x, g, mod, w_qkv, g_q, g_k, w_o):
    s = x.shape[1]
    assert s % MOBA_BLOCK == 0 and s // MOBA_BLOCK <= HEAD_DIM
    e_mat, et_mat = _head_indicator()
    q_scale = LOG2E / math.sqrt(HEAD_DIM)
    gq_row = (jnp.tile(g_q, N_HEADS) * q_scale).reshape(1, D_MODEL)
    gk_row = jnp.tile(g_k, N_HEADS).reshape(1, D_MODEL)
    q, k, v, kbar = _qkv(x, g, mod, w_qkv, gq_row, gk_row, e_mat, et_mat)
    nb = s // MOBA_BLOCK
    kbar = jnp.pad(kbar.reshape(-1, nb, D_MODEL), ((0, 0), (0, HEAD_DIM - nb), (0, 0)))
    selx = _gate(q, kbar, _alibi_query_rows(), jnp.eye(MOBA_BLOCK, dtype=BF16))
    o = _attn(q, selx, k, v, _key_side_table(s))
    return _proj(o, x, mod, w_o)


def kernel(x, c, norm_g, ada_w, ada_b, ffn_w_in, ffn_w_out, attn_w_qkv, attn_g_q, attn_g_k, attn_w_o,
           conv_w_pw1, conv_b_pw1, conv_w_dw, conv_b_dw, conv_ln_g, conv_ln_b, conv_w_pw2, conv_b_pw2):
    depth = norm_g.shape[0]
    b = x.shape[0]
    mods = _ada(c, ada_w, ada_b).reshape(depth, b, N_SUB, 3, D_MODEL)
    for i in range(depth):
        mod = lambda j: mods[i, :, j]
        x = _ffn(x, norm_g[i, 0], mod(0), ffn_w_in[i, 0], ffn_w_out[i, 0])
        m = i // 2
        if i % 2 == 0:
            x = _moba_mixer(x, norm_g[i, 1], mod(1), attn_w_qkv[m], attn_g_q[m], attn_g_k[m], attn_w_o[m])
        else:
            x = _conv(x, norm_g[i, 1], mod(1), conv_w_pw1[m], conv_b_pw1[m], conv_w_dw[m], conv_b_dw[m],
                      conv_ln_g[m], conv_ln_b[m], conv_w_pw2[m], conv_b_pw2[m])
        x = _ffn(x, norm_g[i, 2], mod(2), ffn_w_in[i, 1], ffn_w_out[i, 1])
    return x
```

```python
import functools
import math

import numpy as np
import jax
import jax.numpy as jnp
from jax import lax
from jax.experimental import pallas as pl
from jax.experimental.pallas import tpu as pltpu

D_MODEL = 1024
N_HEADS = 16
HEAD_DIM = 64
MOBA_BLOCK = 256
MOBA_TOPK = 3
CONV_WIDTH = 31
D_FF = 2816
N_SUB = 3
EPS = 1e-6

LANES = 128
SUBLANES = 8
HEADS_PER_LANE_GROUP = LANES // HEAD_DIM
CONV_HALO = 32
MASK_BIAS = -1e30
UNDERFLOW_LOG2 = 160.0
LOG2E = math.log2(math.e)
VMEM_LIMIT = 56 * 1024 * 1024

F32 = jnp.float32
BF16 = jnp.bfloat16


def _cparams(sem):
    return pltpu.CompilerParams(dimension_semantics=sem, vmem_limit_bytes=VMEM_LIMIT)


def _resident(shape, index_map):
    return pl.BlockSpec(shape, index_map, pipeline_mode=pl.Buffered(1))


def _silu(x):
    return x * jax.nn.sigmoid(x)


def _modulated_norm(x, g, mod):
    ms = jnp.mean(x * x, axis=-1, keepdims=True)
    y = x * lax.rsqrt(ms + EPS) * g
    return y * (1.0 + mod[1:2, :]) + mod[0:1, :]


def _ada_kernel(ct_ref, w_ref, b_ref, o_ref, *, batch):
    ca = _silu(ct_ref[...])
    w = w_ref[0]
    for b in range(batch):
        o_ref[0, b:b + 1, :] = jnp.sum(w * ca[:, b:b + 1], axis=0, keepdims=True) + b_ref[0]


def _ada(c, ada_w, ada_b):
    depth, d, n = ada_w.shape
    batch = c.shape[0]
    tn = 1152
    return pl.pallas_call(
        functools.partial(_ada_kernel, batch=batch),
        out_shape=jax.ShapeDtypeStruct((depth, batch, n), F32),
        grid=(depth, n // tn),
        in_specs=[
            pl.BlockSpec((d, batch), lambda i, j: (0, 0)),
            pl.BlockSpec((1, d, tn), lambda i, j: (i, 0, j)),
            pl.BlockSpec((1, 1, tn), lambda i, j: (i, 0, j)),
        ],
        out_specs=pl.BlockSpec((1, batch, tn), lambda i, j: (i, 0, j)),
        compiler_params=_cparams(("arbitrary", "arbitrary")),
        name="adaln",
    )(c.T, ada_w, ada_b.reshape(depth, 1, n))


def _ffn_kernel(x_ref, g_ref, mod_ref, win_ref, wout_ref, o_ref):
    x = x_ref[0]
    mod = mod_ref[0]
    h = _modulated_norm(x, g_ref[...], mod).astype(BF16)
    u = jnp.dot(h, win_ref[...], preferred_element_type=F32)
    act = (_silu(u[:, :D_FF]) * u[:, D_FF:]).astype(BF16)
    y = jnp.dot(act, wout_ref[...], preferred_element_type=F32)
    o_ref[0] = x + (0.5 * mod[2:3, :]) * y


def _ffn(x, g, mod, w_in, w_out, tm=512):
    b, s, d = x.shape
    return pl.pallas_call(
        _ffn_kernel,
        out_shape=jax.ShapeDtypeStruct(x.shape, F32),
        grid=(b, s // tm),
        in_specs=[
            pl.BlockSpec((1, tm, d), lambda i, j: (i, j, 0)),
            pl.BlockSpec((1, d), lambda i, j: (0, 0)),
            pl.BlockSpec((1, 3, d), lambda i, j: (i, 0, 0)),
            _resident((d, 2 * D_FF), lambda i, j: (0, 0)),
            _resident((D_FF, d), lambda i, j: (0, 0)),
        ],
        out_specs=pl.BlockSpec((1, tm, d), lambda i, j: (i, j, 0)),
        compiler_params=_cparams(("parallel", "parallel")),
        name="ffn",
    )(x, g.reshape(1, d), mod, w_in.astype(BF16), w_out.astype(BF16))


def _head_rms_scale(t, e_ref, et_ref):
    ssq = jnp.dot((t * t).astype(BF16), e_ref[...], preferred_element_type=F32)
    r = lax.rsqrt(ssq * (1.0 / HEAD_DIM) + EPS)
    r_hi = r.astype(BF16)
    r_lo = (r - r_hi.astype(F32)).astype(BF16)
    return jnp.dot(jnp.concatenate([r_hi, r_lo], axis=1), et_ref[...], preferred_element_type=F32)


def _qkv_kernel(x_ref, g_ref, mod_ref, w_ref, gq_ref, gk_ref, e_ref, et_ref,
                q_ref, k_ref, v_ref, kbar_ref, *, blocks_per_tile):
    d = D_MODEL
    h = _modulated_norm(x_ref[0], g_ref[...], mod_ref[0]).astype(BF16)
    qkv = jnp.dot(h, w_ref[...], preferred_element_type=F32)
    q, k, v = qkv[:, :d], qkv[:, d:2 * d], qkv[:, 2 * d:]
    qn = q * _head_rms_scale(q, e_ref, et_ref) * gq_ref[...]
    kn = k * _head_rms_scale(k, e_ref, et_ref) * gk_ref[...]
    q_ref[0] = qn.astype(BF16)
    k_ref[0] = kn.astype(BF16)
    v_ref[0] = v.astype(BF16)
    for t in range(blocks_per_tile):
        blk = kn[t * MOBA_BLOCK:(t + 1) * MOBA_BLOCK, :]
        kbar_ref[0, t] = jnp.mean(blk, axis=0, keepdims=True)


def _qkv(x, g, mod, w_qkv, gq_row, gk_row, e_mat, et_mat, tm=512):
    b, s, d = x.shape
    nb = s // MOBA_BLOCK
    bpt = tm // MOBA_BLOCK
    tok = jax.ShapeDtypeStruct((b, s, d), BF16)
    tok_spec = pl.BlockSpec((1, tm, d), lambda i, j: (i, j, 0))
    return pl.pallas_call(
        functools.partial(_qkv_kernel, blocks_per_tile=bpt),
        out_shape=(tok, tok, tok, jax.ShapeDtypeStruct((b, nb, 1, d), F32)),
        grid=(b, s // tm),
        in_specs=[
            tok_spec,
            pl.BlockSpec((1, d), lambda i, j: (0, 0)),
            pl.BlockSpec((1, 3, d), lambda i, j: (i, 0, 0)),
            _resident((d, 3 * d), lambda i, j: (0, 0)),
            pl.BlockSpec((1, d), lambda i, j: (0, 0)),
            pl.BlockSpec((1, d), lambda i, j: (0, 0)),
            _resident((d, LANES), lambda i, j: (0, 0)),
            _resident((2 * LANES, d), lambda i, j: (0, 0)),
        ],
        out_specs=(tok_spec, tok_spec, tok_spec,
                   pl.BlockSpec((1, bpt, 1, d), lambda i, j: (i, j, 0, 0))),
        compiler_params=_cparams(("parallel", "parallel")),
        name="qkv",
    )(x, g.reshape(1, d), mod, w_qkv.astype(BF16), gq_row, gk_row, e_mat, et_mat)


def _gate_kernel(q_ref, kbar_ref, ext_ref, eye_ref, sel_ref):
    own = pl.program_id(1)
    nq = q_ref.shape[1]
    contract_last = (((1,), (1,)), ((), ()))
    blk = lax.broadcasted_iota(jnp.int32, (HEAD_DIM, nq), 0)
    blk_f = blk.astype(F32)
    past = blk < own
    lane = lax.broadcasted_iota(jnp.int32, (HEAD_DIM, LANES), 1)
    for p in range(D_MODEL // LANES):
        qp = q_ref[0, :, p * LANES:(p + 1) * LANES]
        kb = kbar_ref[0, :, p * LANES:(p + 1) * LANES].astype(BF16)
        per_head = [jnp.where((lane >= h * HEAD_DIM) & (lane < (h + 1) * HEAD_DIM), kb, jnp.zeros_like(kb))
                    for h in range(HEADS_PER_LANE_GROUP)]
        gates = lax.dot_general(jnp.concatenate(per_head, axis=0), qp, contract_last,
                                preferred_element_type=F32)
        parts = []
        for h in range(HEADS_PER_LANE_GROUP):
            g = jnp.where(past, gates[h * HEAD_DIM:(h + 1) * HEAD_DIM], -jnp.inf)
            taken = jnp.zeros(g.shape, jnp.bool_)
            for _ in range(MOBA_TOPK):
                gm = jnp.where(taken, -jnp.inf, g)
                m = jnp.max(gm, axis=0, keepdims=True)
                cand = jnp.logical_and(gm == m, jnp.logical_not(taken))
                idx = jnp.min(jnp.where(cand, blk_f, float(HEAD_DIM)), axis=0, keepdims=True)
                taken = jnp.logical_or(taken, blk_f == idx)
            chosen = jnp.logical_and(taken, past)
            parts.append(jnp.where(chosen, 0.0, MASK_BIAS).astype(BF16))
            parts.append(ext_ref[p * HEADS_PER_LANE_GROUP + h])
        out = lax.dot_general(eye_ref[...], jnp.concatenate(parts, axis=0), contract_last,
                              preferred_element_type=F32)
        width = HEADS_PER_LANE_GROUP * LANES
        sel_ref[0, :, p * width:(p + 1) * width] = out.astype(BF16)


def _gate(q, kbar, ext_t, eye):
    b, s, d = q.shape
    nb = s // MOBA_BLOCK
    return pl.pallas_call(
        _gate_kernel,
        out_shape=jax.ShapeDtypeStruct((b, s, N_HEADS * LANES), BF16),
        grid=(b, nb),
        in_specs=[
            pl.BlockSpec((1, MOBA_BLOCK, d), lambda i, j: (i, j, 0)),
            pl.BlockSpec((1, HEAD_DIM, d), lambda i, j: (i, 0, 0)),
            pl.BlockSpec((N_HEADS, HEAD_DIM, MOBA_BLOCK), lambda i, j: (0, 0, 0)),
            pl.BlockSpec((MOBA_BLOCK, MOBA_BLOCK), lambda i, j: (0, 0)),
        ],
        out_specs=pl.BlockSpec((1, MOBA_BLOCK, N_HEADS * LANES), lambda i, j: (i, j, 0)),
        compiler_params=_cparams(("parallel", "parallel")),
        name="moba_gate",
    )(q, kbar, ext_t, eye)


def _attn_kernel(first_it_ref, q_ref, sel_ref, k_ref, v_ref, kx_ref, o_ref, qa_scr, qd_scr, m_scr, acc_scr,
                 *, group, kv_blocks):
    step = pl.program_id(2)
    tq = group * MOBA_BLOCK
    tkv = kv_blocks * MOBA_BLOCK
    chunks = group // kv_blocks
    heads = range(HEADS_PER_LANE_GROUP)
    rows = HEADS_PER_LANE_GROUP * tq
    lane = lax.broadcasted_iota(jnp.int32, (tq, LANES), 1)
    contract_last = (((1,), (1,)), ((), ()))

    def visit(chunk, lhs_scr, first_query=0):
        start = pl.multiple_of(chunk * tkv, tkv)
        ka = jnp.concatenate([k_ref[0, pl.ds(start, tkv), :], kx_ref[pl.ds(start, tkv), :]], axis=1)
        vg = v_ref[0, pl.ds(start, tkv), :]
        kv_lane = lax.broadcasted_iota(jnp.int32, (tkv, LANES), 1)
        va = [jnp.where((kv_lane >= h * HEAD_DIM) & (kv_lane < (h + 1) * HEAD_DIM), vg, jnp.ones_like(vg))
              for h in heads]
        if first_query == 0:
            spans = [(0, rows)]
        else:
            spans = [(h * tq + first_query, (h + 1) * tq) for h in heads]
        for lo, hi in spans:
            s = lax.dot_general(lhs_scr[lo:hi], ka, contract_last, preferred_element_type=F32)
            if lhs_scr is qd_scr:
                query = lax.rem(lax.broadcasted_iota(jnp.int32, s.shape, 0) + lo, tq)
                s = jnp.where(query >= lax.broadcasted_iota(jnp.int32, s.shape, 1) + first_query, s, MASK_BIAS)
            m_prev = m_scr[lo:hi]
            m_new = jnp.maximum(m_prev, jnp.max(s, axis=-1, keepdims=True))
            alpha = jnp.exp2(m_prev - m_new)
            p = jnp.exp2((s - jnp.concatenate([m_new] * (tkv // LANES), axis=1)).astype(BF16))
            pv = [jnp.dot(p[max(lo, h * tq) - lo:min(hi, (h + 1) * tq) - lo], va[h], preferred_element_type=F32)
                  for h in heads if max(lo, h * tq) < min(hi, (h + 1) * tq)]
            acc_scr[lo:hi] = alpha * acc_scr[lo:hi] + jnp.concatenate(pv, axis=0)
            m_scr[lo:hi] = m_new

    qp = q_ref[0]
    row_block = lax.shift_right_logical(lax.broadcasted_iota(jnp.int32, (tq, LANES), 0),
                                        MOBA_BLOCK.bit_length() - 1) + step * group
    for h in heads:
        in_head = (lane >= h * HEAD_DIM) & (lane < (h + 1) * HEAD_DIM)
        qh = jnp.where(in_head, qp, jnp.zeros_like(qp))
        sx = sel_ref[0, :, h * LANES:(h + 1) * LANES]
        qa_scr[h * tq:(h + 1) * tq, :] = jnp.concatenate([qh, sx], axis=1)
        qd_scr[h * tq:(h + 1) * tq, :] = jnp.concatenate(
            [qh, jnp.where(lane == row_block, jnp.zeros_like(sx), sx)], axis=1)
    m_scr[...] = jnp.full((rows, LANES), MASK_BIAS, F32)
    acc_scr[...] = jnp.zeros((rows, LANES), F32)

    for u in range(chunks):
        visit(step * chunks + u, qd_scr, first_query=u * tkv)

    def body(it, carry):
        for u in range(chunks):
            visit(it * chunks + u, qa_scr)
        return carry

    lax.fori_loop(first_it_ref[pl.program_id(1), step], step, body, 0)
    acc = acc_scr[...]
    out = acc / pltpu.roll(acc, HEAD_DIM, axis=1)
    o_ref[0] = jnp.where(lane < HEAD_DIM, out[:tq], out[tq:]).astype(BF16)


def _first_iterations(g_q_scaled, g_k, steps, tq):
    round_up = 1.0 + 2.0 ** -6
    qk = HEAD_DIM * jnp.max(jnp.abs(g_q_scaled)) * jnp.max(jnp.abs(g_k)) * round_up
    slopes = np.array([2.0 ** (-8.0 * (h + 1) / N_HEADS) for h in range(N_HEADS)]) * LOG2E * (1.0 - 2.0 ** -10)
    pair_slope = jnp.asarray(slopes.reshape(-1, HEADS_PER_LANE_GROUP).min(axis=1), F32)
    reach = (UNDERFLOW_LOG2 + 2.0 * qk) / pair_slope
    tile_start = jnp.arange(steps, dtype=F32) * tq
    first = jnp.floor((tile_start[None, :] - reach[:, None]) / tq)
    first = jnp.where(jnp.isfinite(first), first, 0.0)
    return jnp.clip(first, 0.0, jnp.arange(steps, dtype=F32)[None, :]).astype(jnp.int32)


def _attn(q, selx, k, v, kx, g_q_scaled, g_k, group=4, kv_blocks=2):
    b, s, d = q.shape
    pairs = d // LANES
    tq = group * MOBA_BLOCK
    first_it = _first_iterations(g_q_scaled, g_k, s // tq, tq)
    grid_spec = pltpu.PrefetchScalarGridSpec(
        num_scalar_prefetch=1,
        grid=(b, pairs, s // tq),
        in_specs=[
            pl.BlockSpec((1, tq, LANES), lambda bi, p, i, fi: (bi, i, p)),
            pl.BlockSpec((1, tq, HEADS_PER_LANE_GROUP * LANES), lambda bi, p, i, fi: (bi, i, p)),
            pl.BlockSpec((1, s, LANES), lambda bi, p, i, fi: (bi, 0, p)),
            pl.BlockSpec((1, s, LANES), lambda bi, p, i, fi: (bi, 0, p)),
            _resident((s, LANES), lambda bi, p, i, fi: (0, 0)),
        ],
        out_specs=pl.BlockSpec((1, tq, LANES), lambda bi, p, i, fi: (bi, i, p)),
        scratch_shapes=[
            pltpu.VMEM((HEADS_PER_LANE_GROUP * tq, 2 * LANES), BF16),
            pltpu.VMEM((HEADS_PER_LANE_GROUP * tq, 2 * LANES), BF16),
            pltpu.VMEM((HEADS_PER_LANE_GROUP * tq, LANES), F32),
            pltpu.VMEM((HEADS_PER_LANE_GROUP * tq, LANES), F32),
        ],
    )
    return pl.pallas_call(
        functools.partial(_attn_kernel, group=group, kv_blocks=kv_blocks),
        out_shape=jax.ShapeDtypeStruct((b, s, d), BF16),
        grid_spec=grid_spec,
        compiler_params=_cparams(("parallel", "parallel", "arbitrary")),
        name="moba_attn",
    )(first_it, q, selx, k, v, kx)


def _proj_kernel(o_ref, x_ref, mod_ref, w_ref, y_ref):
    y = jnp.dot(o_ref[0], w_ref[...], preferred_element_type=F32)
    y_ref[0] = x_ref[0] + mod_ref[0][2:3, :] * y


def _proj(o, x, mod, w, tm=512):
    b, s, d = x.shape
    tok = lambda i, j: (i, j, 0)
    return pl.pallas_call(
        _proj_kernel,
        out_shape=jax.ShapeDtypeStruct(x.shape, F32),
        grid=(b, s // tm),
        in_specs=[
            pl.BlockSpec((1, tm, d), tok),
            pl.BlockSpec((1, tm, d), tok),
            pl.BlockSpec((1, 3, d), lambda i, j: (i, 0, 0)),
            _resident((d, d), lambda i, j: (0, 0)),
        ],
        out_specs=pl.BlockSpec((1, tm, d), tok),
        compiler_params=_cparams(("parallel", "parallel")),
        name="attn_out_proj",
    )(o, x, mod, w.astype(BF16))


def _conv_kernel(x_ref, g_ref, mod_ref, w1_ref, b1_ref, wdw_ref, bdw_ref, lng_ref, lnb_ref,
                 w2_ref, b2_ref, o_ref, u_scr, c_scr, *, row_chunk):
    d = D_MODEL
    tm = x_ref.shape[1]
    x = x_ref[0]
    mod = mod_ref[0]

    @pl.when(pl.program_id(1) == 0)
    def _():
        u_scr[0:CONV_HALO, :] = jnp.zeros((CONV_HALO, d), F32)

    h = _modulated_norm(x, g_ref[...], mod).astype(BF16)
    u = jnp.dot(h, w1_ref[...], preferred_element_type=F32) + b1_ref[...]
    u_scr[CONV_HALO:CONV_HALO + tm, :] = u[:, :d] * jax.nn.sigmoid(u[:, d:])

    base = CONV_HALO - (CONV_WIDTH - 1)

    def chunk(r, carry):
        r0 = pl.multiple_of(r * row_chunk, row_chunk)
        for lc in range(d // LANES):
            ls = slice(lc * LANES, (lc + 1) * LANES)
            win = u_scr[pl.ds(r0, row_chunk + CONV_HALO), ls]
            acc = jnp.broadcast_to(bdw_ref[:, ls], (row_chunk, LANES))
            for shift in range(SUBLANES):
                taps = [w for w in range(CONV_WIDTH) if (base + w) % SUBLANES == shift]
                shifted = win if shift == 0 else pltpu.roll(win, win.shape[0] - shift, axis=0)
                for w in taps:
                    a0 = (base + w) // SUBLANES * SUBLANES
                    acc = acc + shifted[a0:a0 + row_chunk] * wdw_ref[w:w + 1, ls]
            c_scr[pl.ds(r0, row_chunk), ls] = acc
        return carry

    lax.fori_loop(0, tm // row_chunk, chunk, 0)
    u_scr[0:CONV_HALO, :] = u_scr[tm:tm + CONV_HALO, :]

    c = c_scr[...]
    mu = jnp.mean(c, axis=-1, keepdims=True)
    cc = c - mu
    var = jnp.mean(cc * cc, axis=-1, keepdims=True)
    z = _silu(cc * lax.rsqrt(var + EPS) * lng_ref[...] + lnb_ref[...]).astype(BF16)
    y = jnp.dot(z, w2_ref[...], preferred_element_type=F32) + b2_ref[...]
    o_ref[0] = x + mod[2:3, :] * y


def _conv(x, g, mod, w1, b1, wdw, bdw, lng, lnb, w2, b2, tm=512, row_chunk=64):
    b, s, d = x.shape
    tok = lambda i, j: (i, j, 0)
    vec = lambda n: pl.BlockSpec((1, n), lambda i, j: (0, 0))
    return pl.pallas_call(
        functools.partial(_conv_kernel, row_chunk=row_chunk),
        out_shape=jax.ShapeDtypeStruct(x.shape, F32),
        grid=(b, s // tm),
        in_specs=[
            pl.BlockSpec((1, tm, d), tok),
            vec(d),
            pl.BlockSpec((1, 3, d), lambda i, j: (i, 0, 0)),
            _resident((d, 2 * d), lambda i, j: (0, 0)),
            vec(2 * d),
            pl.BlockSpec((CONV_WIDTH, d), lambda i, j: (0, 0)),
            vec(d), vec(d), vec(d),
            _resident((d, d), lambda i, j: (0, 0)),
            vec(d),
        ],
        out_specs=pl.BlockSpec((1, tm, d), tok),
        scratch_shapes=[pltpu.VMEM((CONV_HALO + tm, d), F32), pltpu.VMEM((tm, d), F32)],
        compiler_params=_cparams(("parallel", "arbitrary")),
        name="conformer_conv",
    )(x, g.reshape(1, d), mod, w1.astype(BF16), b1.reshape(1, 2 * d), wdw, bdw.reshape(1, d),
      lng.reshape(1, d), lnb.reshape(1, d), w2.astype(BF16), b2.reshape(1, d))


def _bf16_pieces(x, n=3):
    out = []
    rest = np.float64(x)
    for _ in range(n):
        p = np.float64(np.asarray(rest, np.float32).astype(jnp.bfloat16).astype(np.float32))
        out.append(float(p))
        rest = rest - p
    return out


def _alibi_query_rows():
    ext = np.zeros((N_HEADS, HEAD_DIM, MOBA_BLOCK), np.float32)
    for h in range(N_HEADS):
        slope = np.float32(2.0 ** (-8.0 * (h + 1) / N_HEADS))
        pieces = _bf16_pieces(float(slope) * LOG2E)
        for e, piece in enumerate(pieces + [MOBA_BLOCK * p for p in pieces]):
            ext[h, e, :] = piece
    return jnp.asarray(ext, dtype=BF16)


def _key_side_table(s):
    pos = np.arange(s)
    blk, off = pos // MOBA_BLOCK, pos % MOBA_BLOCK
    kx = np.zeros((s, LANES), np.float32)
    kx[pos, blk] = 1.0
    kx[:, HEAD_DIM:HEAD_DIM + 3] = off[:, None]
    kx[:, HEAD_DIM + 3:HEAD_DIM + 6] = blk[:, None]
    return jnp.asarray(kx, dtype=BF16)


def _head_indicator():
    e = np.zeros((D_MODEL, LANES), np.float32)
    e[np.arange(D_MODEL), np.arange(D_MODEL) // HEAD_DIM] = 1.0
    et = np.concatenate([e.T, e.T], axis=0)
    return jnp.asarray(e, dtype=BF16), jnp.asarray(et, dtype=BF16)


def _moba_mixer(x, g, mod, w_qkv, g_q, g_k, w_o):
    s = x.shape[1]
    assert s % MOBA_BLOCK == 0 and s // MOBA_BLOCK <= HEAD_DIM
    e_mat, et_mat = _head_indicator()
    q_scale = LOG2E / math.sqrt(HEAD_DIM)
    gq_row = (jnp.tile(g_q, N_HEADS) * q_scale).reshape(1, D_MODEL)
    gk_row = jnp.tile(g_k, N_HEADS).reshape(1, D_MODEL)
    q, k, v, kbar = _qkv(x, g, mod, w_qkv, gq_row, gk_row, e_mat, et_mat)
    nb = s // MOBA_BLOCK
    kbar = jnp.pad(kbar.reshape(-1, nb, D_MODEL), ((0, 0), (0, HEAD_DIM - nb), (0, 0)))
    selx = _gate(q, kbar, _alibi_query_rows(), jnp.eye(MOBA_BLOCK, dtype=BF16))
    o = _attn(q, selx, k, v, _key_side_table(s), g_q * q_scale, g_k)
    return _proj(o, x, mod, w_o)


def kernel(x, c, norm_g, ada_w, ada_b, ffn_w_in, ffn_w_out, attn_w_qkv, attn_g_q, attn_g_k, attn_w_o,
           conv_w_pw1, conv_b_pw1, conv_w_dw, conv_b_dw, conv_ln_g, conv_ln_b, conv_w_pw2, conv_b_pw2):
    depth = norm_g.shape[0]
    b = x.shape[0]
    mods = _ada(c, ada_w, ada_b).reshape(depth, b, N_SUB, 3, D_MODEL)
    for i in range(depth):
        mod = lambda j: mods[i, :, j]
        x = _ffn(x, norm_g[i, 0], mod(0), ffn_w_in[i, 0], ffn_w_out[i, 0])
        m = i // 2
        if i % 2 == 0:
            x = _moba_mixer(x, norm_g[i, 1], mod(1), attn_w_qkv[m], attn_g_q[m], attn_g_k[m], attn_w_o[m])
        else:
            x = _conv(x, norm_g[i, 1], mod(1), conv_w_pw1[m], conv_b_pw1[m], conv_w_dw[m], conv_b_dw[m],
                      conv_ln_g[m], conv_ln_b[m], conv_w_pw2[m], conv_b_pw2[m])
        x = _ffn(x, norm_g[i, 2], mod(2), ffn_w_in[i, 1], ffn_w_out[i, 1])
    return x
```

```python
import functools
import math

import numpy as np
import jax
import jax.numpy as jnp
from jax import lax
from jax.experimental import pallas as pl
from jax.experimental.pallas import tpu as pltpu

D_MODEL = 1024
N_HEADS = 16
HEAD_DIM = 64
MOBA_BLOCK = 256
MOBA_TOPK = 3
CONV_WIDTH = 31
D_FF = 2816
N_SUB = 3
EPS = 1e-6

LANES = 128
SUBLANES = 8
HEADS_PER_LANE_GROUP = LANES // HEAD_DIM
CONV_HALO = 32
MASK_BIAS = -1e30
UNDERFLOW_LOG2 = 160.0
LOG2E = math.log2(math.e)
VMEM_LIMIT = 56 * 1024 * 1024

F32 = jnp.float32
BF16 = jnp.bfloat16


def _cparams(sem):
    return pltpu.CompilerParams(dimension_semantics=sem, vmem_limit_bytes=VMEM_LIMIT)


def _resident(shape, index_map):
    return pl.BlockSpec(shape, index_map, pipeline_mode=pl.Buffered(1))


def _silu(x):
    return x * jax.nn.sigmoid(x)


def _modulated_norm(x, g, mod):
    ms = jnp.mean(x * x, axis=-1, keepdims=True)
    y = x * lax.rsqrt(ms + EPS) * g
    return y * (1.0 + mod[1:2, :]) + mod[0:1, :]


def _ada_kernel(ct_ref, w_ref, b_ref, o_ref, *, batch):
    ca = _silu(ct_ref[...])
    w = w_ref[0]
    for b in range(batch):
        o_ref[0, b:b + 1, :] = jnp.sum(w * ca[:, b:b + 1], axis=0, keepdims=True) + b_ref[0]


def _ada(c, ada_w, ada_b):
    depth, d, n = ada_w.shape
    batch = c.shape[0]
    tn = 1152
    return pl.pallas_call(
        functools.partial(_ada_kernel, batch=batch),
        out_shape=jax.ShapeDtypeStruct((depth, batch, n), F32),
        grid=(depth, n // tn),
        in_specs=[
            pl.BlockSpec((d, batch), lambda i, j: (0, 0)),
            pl.BlockSpec((1, d, tn), lambda i, j: (i, 0, j)),
            pl.BlockSpec((1, 1, tn), lambda i, j: (i, 0, j)),
        ],
        out_specs=pl.BlockSpec((1, batch, tn), lambda i, j: (i, 0, j)),
        compiler_params=_cparams(("arbitrary", "arbitrary")),
        name="adaln",
    )(c.T, ada_w, ada_b.reshape(depth, 1, n))


def _ffn_kernel(*refs, mixer_proj):
    if mixer_proj:
        mo_ref, mw_ref, mmod_ref, x_ref, g_ref, mod_ref, win_ref, wout_ref, o_ref = refs
        x = x_ref[0] + mmod_ref[0][2:3, :] * jnp.dot(mo_ref[0], mw_ref[...], preferred_element_type=F32)
    else:
        x_ref, g_ref, mod_ref, win_ref, wout_ref, o_ref = refs
        x = x_ref[0]
    mod = mod_ref[0]
    h = _modulated_norm(x, g_ref[...], mod).astype(BF16)
    u = jnp.dot(h, win_ref[...], preferred_element_type=F32)
    act = (_silu(u[:, :D_FF]) * u[:, D_FF:]).astype(BF16)
    y = jnp.dot(act, wout_ref[...], preferred_element_type=F32)
    o_ref[0] = x + (0.5 * mod[2:3, :]) * y


def _ffn(x, g, mod, w_in, w_out, mixer=None, tm=512):
    b, s, d = x.shape
    tok = pl.BlockSpec((1, tm, d), lambda i, j: (i, j, 0))
    mod_spec = pl.BlockSpec((1, 3, d), lambda i, j: (i, 0, 0))
    specs = [tok, pl.BlockSpec((1, d), lambda i, j: (0, 0)), mod_spec,
             _resident((d, 2 * D_FF), lambda i, j: (0, 0)), _resident((D_FF, d), lambda i, j: (0, 0))]
    args = [x, g.reshape(1, d), mod, w_in.astype(BF16), w_out.astype(BF16)]
    if mixer is not None:
        o, w_o, mixer_mod = mixer
        specs = [tok, _resident((d, d), lambda i, j: (0, 0)), mod_spec] + specs
        args = [o, w_o.astype(BF16), mixer_mod] + args
    return pl.pallas_call(
        functools.partial(_ffn_kernel, mixer_proj=mixer is not None),
        out_shape=jax.ShapeDtypeStruct(x.shape, F32),
        grid=(b, s // tm),
        in_specs=specs,
        out_specs=tok,
        compiler_params=_cparams(("parallel", "parallel")),
        name="ffn",
    )(*args)


def _head_rms_scale(t, e_ref, et_ref):
    ssq = jnp.dot((t * t).astype(BF16), e_ref[...], preferred_element_type=F32)
    r = lax.rsqrt(ssq * (1.0 / HEAD_DIM) + EPS)
    r_hi = r.astype(BF16)
    r_lo = (r - r_hi.astype(F32)).astype(BF16)
    return jnp.dot(jnp.concatenate([r_hi, r_lo], axis=1), et_ref[...], preferred_element_type=F32)


def _qkv_kernel(x_ref, g_ref, mod_ref, w_ref, gq_ref, gk_ref, e_ref, et_ref,
                q_ref, k_ref, v_ref, kbar_ref, *, blocks_per_tile):
    d = D_MODEL
    h = _modulated_norm(x_ref[0], g_ref[...], mod_ref[0]).astype(BF16)
    qkv = jnp.dot(h, w_ref[...], preferred_element_type=F32)
    q, k, v = qkv[:, :d], qkv[:, d:2 * d], qkv[:, 2 * d:]
    qn = q * _head_rms_scale(q, e_ref, et_ref) * gq_ref[...]
    kn = k * _head_rms_scale(k, e_ref, et_ref) * gk_ref[...]
    q_ref[0] = qn.astype(BF16)
    k_ref[0] = kn.astype(BF16)
    v_ref[0] = v.astype(BF16)
    for t in range(blocks_per_tile):
        blk = kn[t * MOBA_BLOCK:(t + 1) * MOBA_BLOCK, :]
        kbar_ref[0, t] = jnp.mean(blk, axis=0, keepdims=True)


def _qkv(x, g, mod, w_qkv, gq_row, gk_row, e_mat, et_mat, tm=512):
    b, s, d = x.shape
    nb = s // MOBA_BLOCK
    bpt = tm // MOBA_BLOCK
    tok = jax.ShapeDtypeStruct((b, s, d), BF16)
    tok_spec = pl.BlockSpec((1, tm, d), lambda i, j: (i, j, 0))
    return pl.pallas_call(
        functools.partial(_qkv_kernel, blocks_per_tile=bpt),
        out_shape=(tok, tok, tok, jax.ShapeDtypeStruct((b, nb, 1, d), F32)),
        grid=(b, s // tm),
        in_specs=[
            tok_spec,
            pl.BlockSpec((1, d), lambda i, j: (0, 0)),
            pl.BlockSpec((1, 3, d), lambda i, j: (i, 0, 0)),
            _resident((d, 3 * d), lambda i, j: (0, 0)),
            pl.BlockSpec((1, d), lambda i, j: (0, 0)),
            pl.BlockSpec((1, d), lambda i, j: (0, 0)),
            _resident((d, LANES), lambda i, j: (0, 0)),
            _resident((2 * LANES, d), lambda i, j: (0, 0)),
        ],
        out_specs=(tok_spec, tok_spec, tok_spec,
                   pl.BlockSpec((1, bpt, 1, d), lambda i, j: (i, j, 0, 0))),
        compiler_params=_cparams(("parallel", "parallel")),
        name="qkv",
    )(x, g.reshape(1, d), mod, w_qkv.astype(BF16), gq_row, gk_row, e_mat, et_mat)


def _gate_kernel(q_ref, kbar_ref, ext_ref, eye_ref, sel_ref):
    own = pl.program_id(1)
    nq = q_ref.shape[1]
    contract_last = (((1,), (1,)), ((), ()))
    blk = lax.broadcasted_iota(jnp.int32, (HEAD_DIM, nq), 0)
    blk_f = blk.astype(F32)
    past = blk < own
    lane = lax.broadcasted_iota(jnp.int32, (HEAD_DIM, LANES), 1)
    for p in range(D_MODEL // LANES):
        qp = q_ref[0, :, p * LANES:(p + 1) * LANES]
        kb = kbar_ref[0, :, p * LANES:(p + 1) * LANES].astype(BF16)
        per_head = [jnp.where((lane >= h * HEAD_DIM) & (lane < (h + 1) * HEAD_DIM), kb, jnp.zeros_like(kb))
                    for h in range(HEADS_PER_LANE_GROUP)]
        gates = lax.dot_general(jnp.concatenate(per_head, axis=0), qp, contract_last,
                                preferred_element_type=F32)
        parts = []
        for h in range(HEADS_PER_LANE_GROUP):
            g = jnp.where(past, gates[h * HEAD_DIM:(h + 1) * HEAD_DIM], -jnp.inf)
            taken = jnp.zeros(g.shape, jnp.bool_)
            for _ in range(MOBA_TOPK):
                gm = jnp.where(taken, -jnp.inf, g)
                m = jnp.max(gm, axis=0, keepdims=True)
                cand = jnp.logical_and(gm == m, jnp.logical_not(taken))
                idx = jnp.min(jnp.where(cand, blk_f, float(HEAD_DIM)), axis=0, keepdims=True)
                taken = jnp.logical_or(taken, blk_f == idx)
            chosen = jnp.logical_and(taken, past)
            parts.append(jnp.where(chosen, 0.0, MASK_BIAS).astype(BF16))
            parts.append(ext_ref[p * HEADS_PER_LANE_GROUP + h])
        out = lax.dot_general(eye_ref[...], jnp.concatenate(parts, axis=0), contract_last,
                              preferred_element_type=F32)
        width = HEADS_PER_LANE_GROUP * LANES
        sel_ref[0, :, p * width:(p + 1) * width] = out.astype(BF16)


def _gate(q, kbar, ext_t, eye):
    b, s, d = q.shape
    nb = s // MOBA_BLOCK
    return pl.pallas_call(
        _gate_kernel,
        out_shape=jax.ShapeDtypeStruct((b, s, N_HEADS * LANES), BF16),
        grid=(b, nb),
        in_specs=[
            pl.BlockSpec((1, MOBA_BLOCK, d), lambda i, j: (i, j, 0)),
            pl.BlockSpec((1, HEAD_DIM, d), lambda i, j: (i, 0, 0)),
            pl.BlockSpec((N_HEADS, HEAD_DIM, MOBA_BLOCK), lambda i, j: (0, 0, 0)),
            pl.BlockSpec((MOBA_BLOCK, MOBA_BLOCK), lambda i, j: (0, 0)),
        ],
        out_specs=pl.BlockSpec((1, MOBA_BLOCK, N_HEADS * LANES), lambda i, j: (i, j, 0)),
        compiler_params=_cparams(("parallel", "parallel")),
        name="moba_gate",
    )(q, kbar, ext_t, eye)


def _attn_kernel(first_chunk_ref, q_ref, sel_ref, k_ref, v_ref, kx_ref, o_ref, qa_scr, qd_scr, m_scr, acc_scr,
                 *, group, kv_blocks):
    step = pl.program_id(2)
    tq = group * MOBA_BLOCK
    tkv = kv_blocks * MOBA_BLOCK
    chunks = group // kv_blocks
    heads = range(HEADS_PER_LANE_GROUP)
    rows = HEADS_PER_LANE_GROUP * tq
    lane = lax.broadcasted_iota(jnp.int32, (tq, LANES), 1)
    contract_last = (((1,), (1,)), ((), ()))

    def visit(chunk, lhs_scr, first_query=0):
        start = pl.multiple_of(chunk * tkv, tkv)
        ka = jnp.concatenate([k_ref[0, pl.ds(start, tkv), :], kx_ref[pl.ds(start, tkv), :]], axis=1)
        vg = v_ref[0, pl.ds(start, tkv), :]
        kv_lane = lax.broadcasted_iota(jnp.int32, (tkv, LANES), 1)
        va = [jnp.where((kv_lane >= h * HEAD_DIM) & (kv_lane < (h + 1) * HEAD_DIM), vg, jnp.ones_like(vg))
              for h in heads]
        if first_query == 0:
            spans = [(0, rows)]
        else:
            spans = [(h * tq + first_query, (h + 1) * tq) for h in heads]
        for lo, hi in spans:
            s = lax.dot_general(lhs_scr[lo:hi], ka, contract_last, preferred_element_type=F32)
            if lhs_scr is qd_scr:
                query = lax.rem(lax.broadcasted_iota(jnp.int32, s.shape, 0) + lo, tq)
                s = jnp.where(query >= lax.broadcasted_iota(jnp.int32, s.shape, 1) + first_query, s, MASK_BIAS)
            m_prev = m_scr[lo:hi]
            m_new = jnp.maximum(m_prev, jnp.max(s, axis=-1, keepdims=True))
            alpha = jnp.exp2(m_prev - m_new)
            p = jnp.exp2((s - jnp.concatenate([m_new] * (tkv // LANES), axis=1)).astype(BF16))
            pv = [jnp.dot(p[max(lo, h * tq) - lo:min(hi, (h + 1) * tq) - lo], va[h], preferred_element_type=F32)
                  for h in heads if max(lo, h * tq) < min(hi, (h + 1) * tq)]
            acc_scr[lo:hi] = alpha * acc_scr[lo:hi] + jnp.concatenate(pv, axis=0)
            m_scr[lo:hi] = m_new

    qp = q_ref[0]
    row_block = lax.shift_right_logical(lax.broadcasted_iota(jnp.int32, (tq, LANES), 0),
                                        MOBA_BLOCK.bit_length() - 1) + step * group
    for h in heads:
        in_head = (lane >= h * HEAD_DIM) & (lane < (h + 1) * HEAD_DIM)
        qh = jnp.where(in_head, qp, jnp.zeros_like(qp))
        sx = sel_ref[0, :, h * LANES:(h + 1) * LANES]
        qa_scr[h * tq:(h + 1) * tq, :] = jnp.concatenate([qh, sx], axis=1)
        qd_scr[h * tq:(h + 1) * tq, :] = jnp.concatenate(
            [qh, jnp.where(lane == row_block, jnp.zeros_like(sx), sx)], axis=1)
    m_scr[...] = jnp.full((rows, LANES), MASK_BIAS, F32)
    acc_scr[...] = jnp.zeros((rows, LANES), F32)

    for u in range(chunks):
        visit(step * chunks + u, qd_scr, first_query=u * tkv)

    def body(it, carry):
        for u in range(chunks):
            visit(it * chunks + u, qa_scr)
        return carry

    first_chunk = first_chunk_ref[pl.program_id(1), step]
    first_it = lax.div(first_chunk + (chunks - 1), chunks)
    for u in range(chunks - 1):
        @pl.when(first_chunk + u < first_it * chunks)
        def _():
            visit(first_chunk + u, qa_scr)
    lax.fori_loop(first_it, step, body, 0)
    acc = acc_scr[...]
    out = acc / pltpu.roll(acc, HEAD_DIM, axis=1)
    o_ref[0] = jnp.where(lane < HEAD_DIM, out[:tq], out[tq:]).astype(BF16)


def _first_chunks(g_q_scaled, g_k, steps, tq, tkv):
    round_up = 1.0 + 2.0 ** -6
    qk = HEAD_DIM * jnp.max(jnp.abs(g_q_scaled)) * jnp.max(jnp.abs(g_k)) * round_up
    slopes = np.array([2.0 ** (-8.0 * (h + 1) / N_HEADS) for h in range(N_HEADS)]) * LOG2E * (1.0 - 2.0 ** -10)
    pair_slope = jnp.asarray(slopes.reshape(-1, HEADS_PER_LANE_GROUP).min(axis=1), F32)
    reach = (UNDERFLOW_LOG2 + 2.0 * qk) / pair_slope
    tile_start = jnp.arange(steps, dtype=F32) * tq
    first = jnp.floor((tile_start[None, :] - reach[:, None]) / tkv)
    first = jnp.where(jnp.isfinite(first), first, 0.0)
    return jnp.clip(first, 0.0, tile_start[None, :] / tkv).astype(jnp.int32)


def _attn(q, selx, k, v, kx, g_q_scaled, g_k, group=4, kv_blocks=2):
    b, s, d = q.shape
    pairs = d // LANES
    tq = group * MOBA_BLOCK
    first_chunk = _first_chunks(g_q_scaled, g_k, s // tq, tq, kv_blocks * MOBA_BLOCK)
    grid_spec = pltpu.PrefetchScalarGridSpec(
        num_scalar_prefetch=1,
        grid=(b, pairs, s // tq),
        in_specs=[
            pl.BlockSpec((1, tq, LANES), lambda bi, p, i, fi: (bi, i, p)),
            pl.BlockSpec((1, tq, HEADS_PER_LANE_GROUP * LANES), lambda bi, p, i, fi: (bi, i, p)),
            pl.BlockSpec((1, s, LANES), lambda bi, p, i, fi: (bi, 0, p)),
            pl.BlockSpec((1, s, LANES), lambda bi, p, i, fi: (bi, 0, p)),
            _resident((s, LANES), lambda bi, p, i, fi: (0, 0)),
        ],
        out_specs=pl.BlockSpec((1, tq, LANES), lambda bi, p, i, fi: (bi, i, p)),
        scratch_shapes=[
            pltpu.VMEM((HEADS_PER_LANE_GROUP * tq, 2 * LANES), BF16),
            pltpu.VMEM((HEADS_PER_LANE_GROUP * tq, 2 * LANES), BF16),
            pltpu.VMEM((HEADS_PER_LANE_GROUP * tq, LANES), F32),
            pltpu.VMEM((HEADS_PER_LANE_GROUP * tq, LANES), F32),
        ],
    )
    return pl.pallas_call(
        functools.partial(_attn_kernel, group=group, kv_blocks=kv_blocks),
        out_shape=jax.ShapeDtypeStruct((b, s, d), BF16),
        grid_spec=grid_spec,
        compiler_params=_cparams(("parallel", "parallel", "arbitrary")),
        name="moba_attn",
    )(first_chunk, q, selx, k, v, kx)


def _conv_kernel(x_ref, g_ref, mod_ref, w1_ref, b1_ref, wdw_ref, bdw_ref, lng_ref, lnb_ref,
                 w2_ref, b2_ref, o_ref, u_scr, c_scr, *, row_chunk):
    d = D_MODEL
    tm = x_ref.shape[1]
    x = x_ref[0]
    mod = mod_ref[0]

    @pl.when(pl.program_id(1) == 0)
    def _():
        u_scr[0:CONV_HALO, :] = jnp.zeros((CONV_HALO, d), F32)

    h = _modulated_norm(x, g_ref[...], mod).astype(BF16)
    u = jnp.dot(h, w1_ref[...], preferred_element_type=F32) + b1_ref[...]
    u_scr[CONV_HALO:CONV_HALO + tm, :] = u[:, :d] * jax.nn.sigmoid(u[:, d:])

    base = CONV_HALO - (CONV_WIDTH - 1)

    def chunk(r, carry):
        r0 = pl.multiple_of(r * row_chunk, row_chunk)
        for lc in range(d // LANES):
            ls = slice(lc * LANES, (lc + 1) * LANES)
            win = u_scr[pl.ds(r0, row_chunk + CONV_HALO), ls]
            acc = jnp.broadcast_to(bdw_ref[:, ls], (row_chunk, LANES))
            for shift in range(SUBLANES):
                taps = [w for w in range(CONV_WIDTH) if (base + w) % SUBLANES == shift]
                shifted = win if shift == 0 else pltpu.roll(win, win.shape[0] - shift, axis=0)
                for w in taps:
                    a0 = (base + w) // SUBLANES * SUBLANES
                    acc = acc + shifted[a0:a0 + row_chunk] * wdw_ref[w:w + 1, ls]
            c_scr[pl.ds(r0, row_chunk), ls] = acc
        return carry

    lax.fori_loop(0, tm // row_chunk, chunk, 0)
    u_scr[0:CONV_HALO, :] = u_scr[tm:tm + CONV_HALO, :]

    c = c_scr[...]
    mu = jnp.mean(c, axis=-1, keepdims=True)
    cc = c - mu
    var = jnp.mean(cc * cc, axis=-1, keepdims=True)
    z = _silu(cc * lax.rsqrt(var + EPS) * lng_ref[...] + lnb_ref[...]).astype(BF16)
    y = jnp.dot(z, w2_ref[...], preferred_element_type=F32) + b2_ref[...]
    o_ref[0] = x + mod[2:3, :] * y


def _conv(x, g, mod, w1, b1, wdw, bdw, lng, lnb, w2, b2, tm=512, row_chunk=64):
    b, s, d = x.shape
    tok = lambda i, j: (i, j, 0)
    vec = lambda n: pl.BlockSpec((1, n), lambda i, j: (0, 0))
    return pl.pallas_call(
        functools.partial(_conv_kernel, row_chunk=row_chunk),
        out_shape=jax.ShapeDtypeStruct(x.shape, F32),
        grid=(b, s // tm),
        in_specs=[
            pl.BlockSpec((1, tm, d), tok),
            vec(d),
            pl.BlockSpec((1, 3, d), lambda i, j: (i, 0, 0)),
            _resident((d, 2 * d), lambda i, j: (0, 0)),
            vec(2 * d),
            pl.BlockSpec((CONV_WIDTH, d), lambda i, j: (0, 0)),
            vec(d), vec(d), vec(d),
            _resident((d, d), lambda i, j: (0, 0)),
            vec(d),
        ],
        out_specs=pl.BlockSpec((1, tm, d), tok),
        scratch_shapes=[pltpu.VMEM((CONV_HALO + tm, d), F32), pltpu.VMEM((tm, d), F32)],
        compiler_params=_cparams(("parallel", "arbitrary")),
        name="conformer_conv",
    )(x, g.reshape(1, d), mod, w1.astype(BF16), b1.reshape(1, 2 * d), wdw, bdw.reshape(1, d),
      lng.reshape(1, d), lnb.reshape(1, d), w2.astype(BF16), b2.reshape(1, d))


def _bf16_pieces(x, n=3):
    out = []
    rest = np.float64(x)
    for _ in range(n):
        p = np.float64(np.asarray(rest, np.float32).astype(jnp.bfloat16).astype(np.float32))
        out.append(float(p))
        rest = rest - p
    return out


def _alibi_query_rows():
    ext = np.zeros((N_HEADS, HEAD_DIM, MOBA_BLOCK), np.float32)
    for h in range(N_HEADS):
        slope = np.float32(2.0 ** (-8.0 * (h + 1) / N_HEADS))
        pieces = _bf16_pieces(float(slope) * LOG2E)
        for e, piece in enumerate(pieces + [MOBA_BLOCK * p for p in pieces]):
            ext[h, e, :] = piece
    return jnp.asarray(ext, dtype=BF16)


def _key_side_table(s):
    pos = np.arange(s)
    blk, off = pos // MOBA_BLOCK, pos % MOBA_BLOCK
    kx = np.zeros((s, LANES), np.float32)
    kx[pos, blk] = 1.0
    kx[:, HEAD_DIM:HEAD_DIM + 3] = off[:, None]
    kx[:, HEAD_DIM + 3:HEAD_DIM + 6] = blk[:, None]
    return jnp.asarray(kx, dtype=BF16)


def _head_indicator():
    e = np.zeros((D_MODEL, LANES), np.float32)
    e[np.arange(D_MODEL), np.arange(D_MODEL) // HEAD_DIM] = 1.0
    et = np.concatenate([e.T, e.T], axis=0)
    return jnp.asarray(e, dtype=BF16), jnp.asarray(et, dtype=BF16)


def _moba_mixer(x, g, mod, w_qkv, g_q, g_k):
    s = x.shape[1]
    assert s % MOBA_BLOCK == 0 and s // MOBA_BLOCK <= HEAD_DIM
    e_mat, et_mat = _head_indicator()
    q_scale = LOG2E / math.sqrt(HEAD_DIM)
    gq_row = (jnp.tile(g_q, N_HEADS) * q_scale).reshape(1, D_MODEL)
    gk_row = jnp.tile(g_k, N_HEADS).reshape(1, D_MODEL)
    q, k, v, kbar = _qkv(x, g, mod, w_qkv, gq_row, gk_row, e_mat, et_mat)
    nb = s // MOBA_BLOCK
    kbar = jnp.pad(kbar.reshape(-1, nb, D_MODEL), ((0, 0), (0, HEAD_DIM - nb), (0, 0)))
    selx = _gate(q, kbar, _alibi_query_rows(), jnp.eye(MOBA_BLOCK, dtype=BF16))
    return _attn(q, selx, k, v, _key_side_table(s), g_q * q_scale, g_k)


def kernel(x, c, norm_g, ada_w, ada_b, ffn_w_in, ffn_w_out, attn_w_qkv, attn_g_q, attn_g_k, attn_w_o,
           conv_w_pw1, conv_b_pw1, conv_w_dw, conv_b_dw, conv_ln_g, conv_ln_b, conv_w_pw2, conv_b_pw2):
    depth = norm_g.shape[0]
    b = x.shape[0]
    mods = _ada(c, ada_w, ada_b).reshape(depth, b, N_SUB, 3, D_MODEL)
    for i in range(depth):
        mod = lambda j: mods[i, :, j]
        x = _ffn(x, norm_g[i, 0], mod(0), ffn_w_in[i, 0], ffn_w_out[i, 0])
        m = i // 2
        mixer = None
        if i % 2 == 0:
            o = _moba_mixer(x, norm_g[i, 1], mod(1), attn_w_qkv[m], attn_g_q[m], attn_g_k[m])
            mixer = (o, attn_w_o[m], mod(1))
        else:
            x = _conv(x, norm_g[i, 1], mod(1), conv_w_pw1[m], conv_b_pw1[m], conv_w_dw[m], conv_b_dw[m],
                      conv_ln_g[m], conv_ln_b[m], conv_w_pw2[m], conv_b_pw2[m])
        x = _ffn(x, norm_g[i, 2], mod(2), ffn_w_in[i, 1], ffn_w_out[i, 1], mixer=mixer)
    return x
```

```python
import functools
import math

import numpy as np
import jax
import jax.numpy as jnp
from jax import lax
from jax.experimental import pallas as pl
from jax.experimental.pallas import tpu as pltpu

D_MODEL = 1024
N_HEADS = 16
HEAD_DIM = 64
MOBA_BLOCK = 256
MOBA_TOPK = 3
CONV_WIDTH = 31
D_FF = 2816
N_SUB = 3
EPS = 1e-6

LANES = 128
SUBLANES = 8
HEADS_PER_LANE_GROUP = LANES // HEAD_DIM
CONV_HALO = 32
MASK_BIAS = -1e30
UNDERFLOW_LOG2 = 140.0
LOG2E = math.log2(math.e)
VMEM_LIMIT = 56 * 1024 * 1024

F32 = jnp.float32
BF16 = jnp.bfloat16


def _cparams(sem):
    return pltpu.CompilerParams(dimension_semantics=sem, vmem_limit_bytes=VMEM_LIMIT)


def _resident(shape, index_map):
    return pl.BlockSpec(shape, index_map, pipeline_mode=pl.Buffered(1))


def _silu(x):
    return x * jax.nn.sigmoid(x)


def _modulated_norm(x, g, mod):
    ms = jnp.mean(x * x, axis=-1, keepdims=True)
    y = x * lax.rsqrt(ms + EPS) * g
    return y * (1.0 + mod[1:2, :]) + mod[0:1, :]


def _ada_kernel(ct_ref, w_ref, b_ref, o_ref, *, batch):
    ca = _silu(ct_ref[...])
    w = w_ref[0]
    for b in range(batch):
        o_ref[0, b:b + 1, :] = jnp.sum(w * ca[:, b:b + 1], axis=0, keepdims=True) + b_ref[0]


def _ada(c, ada_w, ada_b):
    depth, d, n = ada_w.shape
    batch = c.shape[0]
    tn = 1152
    return pl.pallas_call(
        functools.partial(_ada_kernel, batch=batch),
        out_shape=jax.ShapeDtypeStruct((depth, batch, n), F32),
        grid=(depth, n // tn),
        in_specs=[
            pl.BlockSpec((d, batch), lambda i, j: (0, 0)),
            pl.BlockSpec((1, d, tn), lambda i, j: (i, 0, j)),
            pl.BlockSpec((1, 1, tn), lambda i, j: (i, 0, j)),
        ],
        out_specs=pl.BlockSpec((1, batch, tn), lambda i, j: (i, 0, j)),
        compiler_params=_cparams(("arbitrary", "arbitrary")),
        name="adaln",
    )(c.T, ada_w, ada_b.reshape(depth, 1, n))


def _ffn_kernel(*refs, mixer_proj):
    if mixer_proj:
        mo_ref, mw_ref, mmod_ref, x_ref, g_ref, mod_ref, win_ref, wout_ref, o_ref = refs
        x = x_ref[0] + mmod_ref[0][2:3, :] * jnp.dot(mo_ref[0], mw_ref[...], preferred_element_type=F32)
    else:
        x_ref, g_ref, mod_ref, win_ref, wout_ref, o_ref = refs
        x = x_ref[0]
    mod = mod_ref[0]
    h = _modulated_norm(x, g_ref[...], mod).astype(BF16)
    u = jnp.dot(h, win_ref[...], preferred_element_type=F32)
    act = (_silu(u[:, :D_FF]) * u[:, D_FF:]).astype(BF16)
    y = jnp.dot(act, wout_ref[...], preferred_element_type=F32)
    o_ref[0] = x + (0.5 * mod[2:3, :]) * y


def _ffn(x, g, mod, w_in, w_out, mixer=None, tm=512):
    b, s, d = x.shape
    tok = pl.BlockSpec((1, tm, d), lambda i, j: (i, j, 0))
    mod_spec = pl.BlockSpec((1, 3, d), lambda i, j: (i, 0, 0))
    specs = [tok, pl.BlockSpec((1, d), lambda i, j: (0, 0)), mod_spec,
             _resident((d, 2 * D_FF), lambda i, j: (0, 0)), _resident((D_FF, d), lambda i, j: (0, 0))]
    args = [x, g.reshape(1, d), mod, w_in.astype(BF16), w_out.astype(BF16)]
    if mixer is not None:
        o, w_o, mixer_mod = mixer
        specs = [tok, _resident((d, d), lambda i, j: (0, 0)), mod_spec] + specs
        args = [o, w_o.astype(BF16), mixer_mod] + args
    return pl.pallas_call(
        functools.partial(_ffn_kernel, mixer_proj=mixer is not None),
        out_shape=jax.ShapeDtypeStruct(x.shape, F32),
        grid=(b, s // tm),
        in_specs=specs,
        out_specs=tok,
        compiler_params=_cparams(("parallel", "parallel")),
        name="ffn",
    )(*args)


def _head_rms_scale(t, e_ref, et_ref):
    ssq = jnp.dot((t * t).astype(BF16), e_ref[...], preferred_element_type=F32)
    r = lax.rsqrt(ssq * (1.0 / HEAD_DIM) + EPS)
    r_hi = r.astype(BF16)
    r_lo = (r - r_hi.astype(F32)).astype(BF16)
    return jnp.dot(jnp.concatenate([r_hi, r_lo], axis=1), et_ref[...], preferred_element_type=F32)


def _qkv_kernel(x_ref, g_ref, mod_ref, w_ref, gq_ref, gk_ref, e_ref, et_ref,
                q_ref, k_ref, v_ref, kbar_ref, *, blocks_per_tile):
    d = D_MODEL
    h = _modulated_norm(x_ref[0], g_ref[...], mod_ref[0]).astype(BF16)
    qkv = jnp.dot(h, w_ref[...], preferred_element_type=F32)
    q, k, v = qkv[:, :d], qkv[:, d:2 * d], qkv[:, 2 * d:]
    qn = q * _head_rms_scale(q, e_ref, et_ref) * gq_ref[...]
    kn = k * _head_rms_scale(k, e_ref, et_ref) * gk_ref[...]
    q_ref[0] = qn.astype(BF16)
    k_ref[0] = kn.astype(BF16)
    v_ref[0] = v.astype(BF16)
    for t in range(blocks_per_tile):
        blk = kn[t * MOBA_BLOCK:(t + 1) * MOBA_BLOCK, :]
        kbar_ref[0, t] = jnp.mean(blk, axis=0, keepdims=True)


def _qkv(x, g, mod, w_qkv, gq_row, gk_row, e_mat, et_mat, tm=512):
    b, s, d = x.shape
    nb = s // MOBA_BLOCK
    bpt = tm // MOBA_BLOCK
    tok = jax.ShapeDtypeStruct((b, s, d), BF16)
    tok_spec = pl.BlockSpec((1, tm, d), lambda i, j: (i, j, 0))
    return pl.pallas_call(
        functools.partial(_qkv_kernel, blocks_per_tile=bpt),
        out_shape=(tok, tok, tok, jax.ShapeDtypeStruct((b, nb, 1, d), F32)),
        grid=(b, s // tm),
        in_specs=[
            tok_spec,
            pl.BlockSpec((1, d), lambda i, j: (0, 0)),
            pl.BlockSpec((1, 3, d), lambda i, j: (i, 0, 0)),
            _resident((d, 3 * d), lambda i, j: (0, 0)),
            pl.BlockSpec((1, d), lambda i, j: (0, 0)),
            pl.BlockSpec((1, d), lambda i, j: (0, 0)),
            _resident((d, LANES), lambda i, j: (0, 0)),
            _resident((2 * LANES, d), lambda i, j: (0, 0)),
        ],
        out_specs=(tok_spec, tok_spec, tok_spec,
                   pl.BlockSpec((1, bpt, 1, d), lambda i, j: (i, j, 0, 0))),
        compiler_params=_cparams(("parallel", "parallel")),
        name="qkv",
    )(x, g.reshape(1, d), mod, w_qkv.astype(BF16), gq_row, gk_row, e_mat, et_mat)


def _gate_kernel(q_ref, kbar_ref, ext_ref, eye_ref, sel_ref):
    own = pl.program_id(1)
    nq = q_ref.shape[1]
    contract_last = (((1,), (1,)), ((), ()))
    blk = lax.broadcasted_iota(jnp.int32, (HEAD_DIM, nq), 0)
    blk_f = blk.astype(F32)
    past = blk < own
    lane = lax.broadcasted_iota(jnp.int32, (HEAD_DIM, LANES), 1)
    for p in range(D_MODEL // LANES):
        qp = q_ref[0, :, p * LANES:(p + 1) * LANES]
        kb = kbar_ref[0, :, p * LANES:(p + 1) * LANES].astype(BF16)
        per_head = [jnp.where((lane >= h * HEAD_DIM) & (lane < (h + 1) * HEAD_DIM), kb, jnp.zeros_like(kb))
                    for h in range(HEADS_PER_LANE_GROUP)]
        gates = lax.dot_general(jnp.concatenate(per_head, axis=0), qp, contract_last,
                                preferred_element_type=F32)
        parts = []
        for h in range(HEADS_PER_LANE_GROUP):
            g = jnp.where(past, gates[h * HEAD_DIM:(h + 1) * HEAD_DIM], -jnp.inf)
            taken = jnp.zeros(g.shape, jnp.bool_)
            for _ in range(MOBA_TOPK):
                gm = jnp.where(taken, -jnp.inf, g)
                m = jnp.max(gm, axis=0, keepdims=True)
                cand = jnp.logical_and(gm == m, jnp.logical_not(taken))
                idx = jnp.min(jnp.where(cand, blk_f, float(HEAD_DIM)), axis=0, keepdims=True)
                taken = jnp.logical_or(taken, blk_f == idx)
            chosen = jnp.logical_and(taken, past)
            parts.append(jnp.where(chosen, 0.0, MASK_BIAS).astype(BF16))
            parts.append(ext_ref[p * HEADS_PER_LANE_GROUP + h])
        out = lax.dot_general(eye_ref[...], jnp.concatenate(parts, axis=0), contract_last,
                              preferred_element_type=F32)
        width = HEADS_PER_LANE_GROUP * LANES
        sel_ref[0, :, p * width:(p + 1) * width] = out.astype(BF16)


def _gate(q, kbar, ext_t, eye):
    b, s, d = q.shape
    nb = s // MOBA_BLOCK
    return pl.pallas_call(
        _gate_kernel,
        out_shape=jax.ShapeDtypeStruct((b, s, N_HEADS * LANES), BF16),
        grid=(b, nb),
        in_specs=[
            pl.BlockSpec((1, MOBA_BLOCK, d), lambda i, j: (i, j, 0)),
            pl.BlockSpec((1, HEAD_DIM, d), lambda i, j: (i, 0, 0)),
            pl.BlockSpec((N_HEADS, HEAD_DIM, MOBA_BLOCK), lambda i, j: (0, 0, 0)),
            pl.BlockSpec((MOBA_BLOCK, MOBA_BLOCK), lambda i, j: (0, 0)),
        ],
        out_specs=pl.BlockSpec((1, MOBA_BLOCK, N_HEADS * LANES), lambda i, j: (i, j, 0)),
        compiler_params=_cparams(("parallel", "parallel")),
        name="moba_gate",
    )(q, kbar, ext_t, eye)


def _attn_kernel(first_chunk_ref, q_ref, sel_ref, k_ref, v_ref, kx_ref, o_ref, qa_scr, qd_scr, m_scr, acc_scr,
                 *, group, kv_blocks):
    step = pl.program_id(2)
    tq = group * MOBA_BLOCK
    tkv = kv_blocks * MOBA_BLOCK
    chunks = group // kv_blocks
    heads = range(HEADS_PER_LANE_GROUP)
    rows = HEADS_PER_LANE_GROUP * tq
    lane = lax.broadcasted_iota(jnp.int32, (tq, LANES), 1)
    contract_last = (((1,), (1,)), ((), ()))

    def visit(chunk, lhs_scr, first_query=0):
        start = pl.multiple_of(chunk * tkv, tkv)
        ka = jnp.concatenate([k_ref[0, pl.ds(start, tkv), :], kx_ref[pl.ds(start, tkv), :]], axis=1)
        vg = v_ref[0, pl.ds(start, tkv), :]
        kv_lane = lax.broadcasted_iota(jnp.int32, (tkv, LANES), 1)
        va = [jnp.where((kv_lane >= h * HEAD_DIM) & (kv_lane < (h + 1) * HEAD_DIM), vg, jnp.ones_like(vg))
              for h in heads]
        if first_query == 0:
            spans = [(0, rows)]
        else:
            spans = [(h * tq + first_query, (h + 1) * tq) for h in heads]
        for lo, hi in spans:
            s = lax.dot_general(lhs_scr[lo:hi], ka, contract_last, preferred_element_type=F32)
            if lhs_scr is qd_scr:
                query = lax.rem(lax.broadcasted_iota(jnp.int32, s.shape, 0) + lo, tq)
                s = jnp.where(query >= lax.broadcasted_iota(jnp.int32, s.shape, 1) + first_query, s, MASK_BIAS)
            m_prev = m_scr[lo:hi]
            m_new = jnp.maximum(m_prev, jnp.max(s, axis=-1, keepdims=True))
            alpha = jnp.exp2(m_prev - m_new)
            p = jnp.exp2((s - jnp.concatenate([m_new] * (tkv // LANES), axis=1)).astype(BF16))
            pv = [jnp.dot(p[max(lo, h * tq) - lo:min(hi, (h + 1) * tq) - lo], va[h], preferred_element_type=F32)
                  for h in heads if max(lo, h * tq) < min(hi, (h + 1) * tq)]
            acc_scr[lo:hi] = alpha * acc_scr[lo:hi] + jnp.concatenate(pv, axis=0)
            m_scr[lo:hi] = m_new

    qp = q_ref[0]
    row_block = lax.shift_right_logical(lax.broadcasted_iota(jnp.int32, (tq, LANES), 0),
                                        MOBA_BLOCK.bit_length() - 1) + step * group
    for h in heads:
        in_head = (lane >= h * HEAD_DIM) & (lane < (h + 1) * HEAD_DIM)
        qh = jnp.where(in_head, qp, jnp.zeros_like(qp))
        sx = sel_ref[0, :, h * LANES:(h + 1) * LANES]
        qa_scr[h * tq:(h + 1) * tq, :] = jnp.concatenate([qh, sx], axis=1)
        qd_scr[h * tq:(h + 1) * tq, :] = jnp.concatenate(
            [qh, jnp.where(lane == row_block, jnp.zeros_like(sx), sx)], axis=1)
    m_scr[...] = jnp.full((rows, LANES), MASK_BIAS, F32)
    acc_scr[...] = jnp.zeros((rows, LANES), F32)

    for u in range(chunks):
        visit(step * chunks + u, qd_scr, first_query=u * tkv)

    def body(it, carry):
        for u in range(chunks):
            visit(it * chunks + u, qa_scr)
        return carry

    first_chunk = first_chunk_ref[pl.program_id(1), step]
    first_it = lax.div(first_chunk + (chunks - 1), chunks)
    for u in range(chunks - 1):
        @pl.when(first_chunk + u < first_it * chunks)
        def _():
            visit(first_chunk + u, qa_scr)
    lax.fori_loop(first_it, step, body, 0)
    acc = acc_scr[...]
    out = acc / pltpu.roll(acc, HEAD_DIM, axis=1)
    o_ref[0] = jnp.where(lane < HEAD_DIM, out[:tq], out[tq:]).astype(BF16)


def _first_chunks(g_q_scaled, g_k, steps, tq, tkv):
    round_up = 1.0 + 2.0 ** -6
    qk = HEAD_DIM * jnp.max(jnp.abs(g_q_scaled)) * jnp.max(jnp.abs(g_k)) * round_up
    slopes = np.array([2.0 ** (-8.0 * (h + 1) / N_HEADS) for h in range(N_HEADS)]) * LOG2E * (1.0 - 2.0 ** -10)
    pair_slope = jnp.asarray(slopes.reshape(-1, HEADS_PER_LANE_GROUP).min(axis=1), F32)
    reach = (UNDERFLOW_LOG2 + 2.0 * qk) / pair_slope
    tile_start = jnp.arange(steps, dtype=F32) * tq
    first = jnp.floor((tile_start[None, :] - reach[:, None]) / tkv)
    first = jnp.where(jnp.isfinite(first), first, 0.0)
    return jnp.clip(first, 0.0, tile_start[None, :] / tkv).astype(jnp.int32)


def _attn(q, selx, k, v, kx, g_q_scaled, g_k, group=4, kv_blocks=2):
    b, s, d = q.shape
    pairs = d // LANES
    tq = group * MOBA_BLOCK
    first_chunk = _first_chunks(g_q_scaled, g_k, s // tq, tq, kv_blocks * MOBA_BLOCK)
    grid_spec = pltpu.PrefetchScalarGridSpec(
        num_scalar_prefetch=1,
        grid=(b, pairs, s // tq),
        in_specs=[
            pl.BlockSpec((1, tq, LANES), lambda bi, p, i, fi: (bi, i, p)),
            pl.BlockSpec((1, tq, HEADS_PER_LANE_GROUP * LANES), lambda bi, p, i, fi: (bi, i, p)),
            pl.BlockSpec((1, s, LANES), lambda bi, p, i, fi: (bi, 0, p)),
            pl.BlockSpec((1, s, LANES), lambda bi, p, i, fi: (bi, 0, p)),
            _resident((s, LANES), lambda bi, p, i, fi: (0, 0)),
        ],
        out_specs=pl.BlockSpec((1, tq, LANES), lambda bi, p, i, fi: (bi, i, p)),
        scratch_shapes=[
            pltpu.VMEM((HEADS_PER_LANE_GROUP * tq, 2 * LANES), BF16),
            pltpu.VMEM((HEADS_PER_LANE_GROUP * tq, 2 * LANES), BF16),
            pltpu.VMEM((HEADS_PER_LANE_GROUP * tq, LANES), F32),
            pltpu.VMEM((HEADS_PER_LANE_GROUP * tq, LANES), F32),
        ],
    )
    return pl.pallas_call(
        functools.partial(_attn_kernel, group=group, kv_blocks=kv_blocks),
        out_shape=jax.ShapeDtypeStruct((b, s, d), BF16),
        grid_spec=grid_spec,
        compiler_params=_cparams(("parallel", "parallel", "arbitrary")),
        name="moba_attn",
    )(first_chunk, q, selx, k, v, kx)


def _conv_kernel(x_ref, g_ref, mod_ref, w1_ref, b1_ref, wdw_ref, bdw_ref, lng_ref, lnb_ref,
                 w2_ref, b2_ref, o_ref, u_scr, c_scr, *, row_chunk):
    d = D_MODEL
    tm = x_ref.shape[1]
    x = x_ref[0]
    mod = mod_ref[0]

    @pl.when(pl.program_id(1) == 0)
    def _():
        u_scr[0:CONV_HALO, :] = jnp.zeros((CONV_HALO, d), F32)

    h = _modulated_norm(x, g_ref[...], mod).astype(BF16)
    u = jnp.dot(h, w1_ref[...], preferred_element_type=F32) + b1_ref[...]
    u_scr[CONV_HALO:CONV_HALO + tm, :] = u[:, :d] * jax.nn.sigmoid(u[:, d:])

    base = CONV_HALO - (CONV_WIDTH - 1)

    def chunk(r, carry):
        r0 = pl.multiple_of(r * row_chunk, row_chunk)
        for lc in range(d // LANES):
            ls = slice(lc * LANES, (lc + 1) * LANES)
            win = u_scr[pl.ds(r0, row_chunk + CONV_HALO), ls]
            acc = jnp.broadcast_to(bdw_ref[:, ls], (row_chunk, LANES))
            for shift in range(SUBLANES):
                taps = [w for w in range(CONV_WIDTH) if (base + w) % SUBLANES == shift]
                shifted = win if shift == 0 else pltpu.roll(win, win.shape[0] - shift, axis=0)
                for w in taps:
                    a0 = (base + w) // SUBLANES * SUBLANES
                    acc = acc + shifted[a0:a0 + row_chunk] * wdw_ref[w:w + 1, ls]
            c_scr[pl.ds(r0, row_chunk), ls] = acc
        return carry

    lax.fori_loop(0, tm // row_chunk, chunk, 0)
    u_scr[0:CONV_HALO, :] = u_scr[tm:tm + CONV_HALO, :]

    c = c_scr[...]
    mu = jnp.mean(c, axis=-1, keepdims=True)
    cc = c - mu
    var = jnp.mean(cc * cc, axis=-1, keepdims=True)
    z = _silu(cc * lax.rsqrt(var + EPS) * lng_ref[...] + lnb_ref[...]).astype(BF16)
    y = jnp.dot(z, w2_ref[...], preferred_element_type=F32) + b2_ref[...]
    o_ref[0] = x + mod[2:3, :] * y


def _conv(x, g, mod, w1, b1, wdw, bdw, lng, lnb, w2, b2, tm=512, row_chunk=64):
    b, s, d = x.shape
    tok = lambda i, j: (i, j, 0)
    vec = lambda n: pl.BlockSpec((1, n), lambda i, j: (0, 0))
    return pl.pallas_call(
        functools.partial(_conv_kernel, row_chunk=row_chunk),
        out_shape=jax.ShapeDtypeStruct(x.shape, F32),
        grid=(b, s // tm),
        in_specs=[
            pl.BlockSpec((1, tm, d), tok),
            vec(d),
            pl.BlockSpec((1, 3, d), lambda i, j: (i, 0, 0)),
            _resident((d, 2 * d), lambda i, j: (0, 0)),
            vec(2 * d),
            pl.BlockSpec((CONV_WIDTH, d), lambda i, j: (0, 0)),
            vec(d), vec(d), vec(d),
            _resident((d, d), lambda i, j: (0, 0)),
            vec(d),
        ],
        out_specs=pl.BlockSpec((1, tm, d), tok),
        scratch_shapes=[pltpu.VMEM((CONV_HALO + tm, d), F32), pltpu.VMEM((tm, d), F32)],
        compiler_params=_cparams(("parallel", "arbitrary")),
        name="conformer_conv",
    )(x, g.reshape(1, d), mod, w1.astype(BF16), b1.reshape(1, 2 * d), wdw, bdw.reshape(1, d),
      lng.reshape(1, d), lnb.reshape(1, d), w2.astype(BF16), b2.reshape(1, d))


def _bf16_pieces(x, n=3):
    out = []
    rest = np.float64(x)
    for _ in range(n):
        p = np.float64(np.asarray(rest, np.float32).astype(jnp.bfloat16).astype(np.float32))
        out.append(float(p))
        rest = rest - p
    return out


def _alibi_query_rows():
    ext = np.zeros((N_HEADS, HEAD_DIM, MOBA_BLOCK), np.float32)
    for h in range(N_HEADS):
        slope = np.float32(2.0 ** (-8.0 * (h + 1) / N_HEADS))
        pieces = _bf16_pieces(float(slope) * LOG2E)
        for e, piece in enumerate(pieces + [MOBA_BLOCK * p for p in pieces]):
            ext[h, e, :] = piece
    return jnp.asarray(ext, dtype=BF16)


def _key_side_table(s):
    pos = np.arange(s)
    blk, off = pos // MOBA_BLOCK, pos % MOBA_BLOCK
    kx = np.zeros((s, LANES), np.float32)
    kx[pos, blk] = 1.0
    kx[:, HEAD_DIM:HEAD_DIM + 3] = off[:, None]
    kx[:, HEAD_DIM + 3:HEAD_DIM + 6] = blk[:, None]
    return jnp.asarray(kx, dtype=BF16)


def _head_indicator():
    e = np.zeros((D_MODEL, LANES), np.float32)
    e[np.arange(D_MODEL), np.arange(D_MODEL) // HEAD_DIM] = 1.0
    et = np.concatenate([e.T, e.T], axis=0)
    return jnp.asarray(e, dtype=BF16), jnp.asarray(et, dtype=BF16)


def _moba_mixer(x, g, mod, w_qkv, g_q, g_k):
    s = x.shape[1]
    assert s % MOBA_BLOCK == 0 and s // MOBA_BLOCK <= HEAD_DIM
    e_mat, et_mat = _head_indicator()
    q_scale = LOG2E / math.sqrt(HEAD_DIM)
    gq_row = (jnp.tile(g_q, N_HEADS) * q_scale).reshape(1, D_MODEL)
    gk_row = jnp.tile(g_k, N_HEADS).reshape(1, D_MODEL)
    q, k, v, kbar = _qkv(x, g, mod, w_qkv, gq_row, gk_row, e_mat, et_mat)
    nb = s // MOBA_BLOCK
    kbar = jnp.pad(kbar.reshape(-1, nb, D_MODEL), ((0, 0), (0, HEAD_DIM - nb), (0, 0)))
    selx = _gate(q, kbar, _alibi_query_rows(), jnp.eye(MOBA_BLOCK, dtype=BF16))
    return _attn(q, selx, k, v, _key_side_table(s), g_q * q_scale, g_k)


def kernel(x, c, norm_g, ada_w, ada_b, ffn_w_in, ffn_w_out, attn_w_qkv, attn_g_q, attn_g_k, attn_w_o,
           conv_w_pw1, conv_b_pw1, conv_w_dw, conv_b_dw, conv_ln_g, conv_ln_b, conv_w_pw2, conv_b_pw2):
    depth = norm_g.shape[0]
    b = x.shape[0]
    mods = _ada(c, ada_w, ada_b).reshape(depth, b, N_SUB, 3, D_MODEL)
    for i in range(depth):
        mod = lambda j: mods[i, :, j]
        x = _ffn(x, norm_g[i, 0], mod(0), ffn_w_in[i, 0], ffn_w_out[i, 0])
        m = i // 2
        mixer = None
        if i % 2 == 0:
            o = _moba_mixer(x, norm_g[i, 1], mod(1), attn_w_qkv[m], attn_g_q[m], attn_g_k[m])
            mixer = (o, attn_w_o[m], mod(1))
        else:
            x = _conv(x, norm_g[i, 1], mod(1), conv_w_pw1[m], conv_b_pw1[m], conv_w_dw[m], conv_b_dw[m],
                      conv_ln_g[m], conv_ln_b[m], conv_w_pw2[m], conv_b_pw2[m])
        x = _ffn(x, norm_g[i, 2], mod(2), ffn_w_in[i, 1], ffn_w_out[i, 1], mixer=mixer)
    return x
```

```python
import functools
import math

import numpy as np
import jax
import jax.numpy as jnp
from jax import lax
from jax.experimental import pallas as pl
from jax.experimental.pallas import tpu as pltpu

D_MODEL = 1024
N_HEADS = 16
HEAD_DIM = 64
MOBA_BLOCK = 256
MOBA_TOPK = 3
CONV_WIDTH = 31
D_FF = 2816
N_SUB = 3
EPS = 1e-6

LANES = 128
SUBLANES = 8
HEADS_PER_LANE_GROUP = LANES // HEAD_DIM
CONV_HALO = 32
MASK_BIAS = -1e30
UNDERFLOW_LOG2 = 140.0
LOG2E = math.log2(math.e)
VMEM_LIMIT = 56 * 1024 * 1024

F32 = jnp.float32
BF16 = jnp.bfloat16


def _cparams(sem):
    return pltpu.CompilerParams(dimension_semantics=sem, vmem_limit_bytes=VMEM_LIMIT)


def _resident(shape, index_map):
    return pl.BlockSpec(shape, index_map, pipeline_mode=pl.Buffered(1))


def _silu(x):
    return x * jax.nn.sigmoid(x)


def _modulated_norm(x, g, mod):
    ms = jnp.mean(x * x, axis=-1, keepdims=True)
    y = x * lax.rsqrt(ms + EPS) * g
    return y * (1.0 + mod[1:2, :]) + mod[0:1, :]


def _ada_kernel(ct_ref, w_ref, b_ref, o_ref, *, batch):
    ca = _silu(ct_ref[...])
    w = w_ref[0]
    for b in range(batch):
        o_ref[0, b:b + 1, :] = jnp.sum(w * ca[:, b:b + 1], axis=0, keepdims=True) + b_ref[0]


def _ada(c, ada_w, ada_b):
    depth, d, n = ada_w.shape
    batch = c.shape[0]
    tn = 1152
    return pl.pallas_call(
        functools.partial(_ada_kernel, batch=batch),
        out_shape=jax.ShapeDtypeStruct((depth, batch, n), F32),
        grid=(depth, n // tn),
        in_specs=[
            pl.BlockSpec((d, batch), lambda i, j: (0, 0)),
            pl.BlockSpec((1, d, tn), lambda i, j: (i, 0, j)),
            pl.BlockSpec((1, 1, tn), lambda i, j: (i, 0, j)),
        ],
        out_specs=pl.BlockSpec((1, batch, tn), lambda i, j: (i, 0, j)),
        compiler_params=_cparams(("arbitrary", "arbitrary")),
        name="adaln",
    )(c.T, ada_w, ada_b.reshape(depth, 1, n))


def _ffn_kernel(*refs, mixer_proj):
    if mixer_proj:
        mo_ref, mw_ref, mmod_ref, x_ref, g_ref, mod_ref, win_ref, wout_ref, o_ref = refs
        x = x_ref[0] + mmod_ref[0][2:3, :] * jnp.dot(mo_ref[0], mw_ref[...], preferred_element_type=F32)
    else:
        x_ref, g_ref, mod_ref, win_ref, wout_ref, o_ref = refs
        x = x_ref[0]
    mod = mod_ref[0]
    h = _modulated_norm(x, g_ref[...], mod).astype(BF16)
    u = jnp.dot(h, win_ref[...], preferred_element_type=F32)
    act = (_silu(u[:, :D_FF]) * u[:, D_FF:]).astype(BF16)
    y = jnp.dot(act, wout_ref[...], preferred_element_type=F32)
    o_ref[0] = x + (0.5 * mod[2:3, :]) * y


def _ffn(x, g, mod, w_in_all, w_out_all, which, mixer=None, tm=512):
    b, s, d = x.shape
    layer, slot = which
    tok = pl.BlockSpec((1, tm, d), lambda i, j: (i, j, 0))
    mod_spec = pl.BlockSpec((1, 3, d), lambda i, j: (i, 0, 0))
    specs = [tok, pl.BlockSpec((1, d), lambda i, j: (0, 0)), mod_spec,
             _resident((None, None, d, 2 * D_FF), lambda i, j: (layer, slot, 0, 0)),
             _resident((None, None, D_FF, d), lambda i, j: (layer, slot, 0, 0))]
    args = [x, g.reshape(1, d), mod, w_in_all, w_out_all]
    if mixer is not None:
        o, w_o, mixer_mod = mixer
        specs = [tok, _resident((d, d), lambda i, j: (0, 0)), mod_spec] + specs
        args = [o, w_o.astype(BF16), mixer_mod] + args
    return pl.pallas_call(
        functools.partial(_ffn_kernel, mixer_proj=mixer is not None),
        out_shape=jax.ShapeDtypeStruct(x.shape, F32),
        grid=(b, s // tm),
        in_specs=specs,
        out_specs=tok,
        compiler_params=_cparams(("parallel", "parallel")),
        name="ffn",
    )(*args)


def _head_rms_scale(t, e_ref, et_ref):
    ssq = jnp.dot((t * t).astype(BF16), e_ref[...], preferred_element_type=F32)
    r = lax.rsqrt(ssq * (1.0 / HEAD_DIM) + EPS)
    r_hi = r.astype(BF16)
    r_lo = (r - r_hi.astype(F32)).astype(BF16)
    return jnp.dot(jnp.concatenate([r_hi, r_lo], axis=1), et_ref[...], preferred_element_type=F32)


def _qkv_kernel(x_ref, g_ref, mod_ref, w_ref, gq_ref, gk_ref, e_ref, et_ref,
                q_ref, k_ref, v_ref, kbar_ref, *, blocks_per_tile):
    d = D_MODEL
    h = _modulated_norm(x_ref[0], g_ref[...], mod_ref[0]).astype(BF16)
    qkv = jnp.dot(h, w_ref[...], preferred_element_type=F32)
    q, k, v = qkv[:, :d], qkv[:, d:2 * d], qkv[:, 2 * d:]
    qn = q * _head_rms_scale(q, e_ref, et_ref) * gq_ref[...]
    kn = k * _head_rms_scale(k, e_ref, et_ref) * gk_ref[...]
    q_ref[0] = qn.astype(BF16)
    k_ref[0] = kn.astype(BF16)
    v_ref[0] = v.astype(BF16)
    for t in range(blocks_per_tile):
        blk = kn[t * MOBA_BLOCK:(t + 1) * MOBA_BLOCK, :]
        kbar_ref[0, t] = jnp.mean(blk, axis=0, keepdims=True)


def _qkv(x, g, mod, w_qkv, gq_row, gk_row, e_mat, et_mat, tm=512):
    b, s, d = x.shape
    nb = s // MOBA_BLOCK
    bpt = tm // MOBA_BLOCK
    tok = jax.ShapeDtypeStruct((b, s, d), BF16)
    tok_spec = pl.BlockSpec((1, tm, d), lambda i, j: (i, j, 0))
    return pl.pallas_call(
        functools.partial(_qkv_kernel, blocks_per_tile=bpt),
        out_shape=(tok, tok, tok, jax.ShapeDtypeStruct((b, nb, 1, d), F32)),
        grid=(b, s // tm),
        in_specs=[
            tok_spec,
            pl.BlockSpec((1, d), lambda i, j: (0, 0)),
            pl.BlockSpec((1, 3, d), lambda i, j: (i, 0, 0)),
            _resident((d, 3 * d), lambda i, j: (0, 0)),
            pl.BlockSpec((1, d), lambda i, j: (0, 0)),
            pl.BlockSpec((1, d), lambda i, j: (0, 0)),
            _resident((d, LANES), lambda i, j: (0, 0)),
            _resident((2 * LANES, d), lambda i, j: (0, 0)),
        ],
        out_specs=(tok_spec, tok_spec, tok_spec,
                   pl.BlockSpec((1, bpt, 1, d), lambda i, j: (i, j, 0, 0))),
        compiler_params=_cparams(("parallel", "parallel")),
        name="qkv",
    )(x, g.reshape(1, d), mod, w_qkv.astype(BF16), gq_row, gk_row, e_mat, et_mat)


def _gate_kernel(q_ref, kbar_ref, ext_ref, eye_ref, sel_ref):
    own = pl.program_id(1)
    nq = q_ref.shape[1]
    contract_last = (((1,), (1,)), ((), ()))
    blk = lax.broadcasted_iota(jnp.int32, (HEAD_DIM, nq), 0)
    blk_f = blk.astype(F32)
    past = blk < own
    lane = lax.broadcasted_iota(jnp.int32, (HEAD_DIM, LANES), 1)
    for p in range(D_MODEL // LANES):
        qp = q_ref[0, :, p * LANES:(p + 1) * LANES]
        kb = kbar_ref[0, :, p * LANES:(p + 1) * LANES].astype(BF16)
        per_head = [jnp.where((lane >= h * HEAD_DIM) & (lane < (h + 1) * HEAD_DIM), kb, jnp.zeros_like(kb))
                    for h in range(HEADS_PER_LANE_GROUP)]
        gates = lax.dot_general(jnp.concatenate(per_head, axis=0), qp, contract_last,
                                preferred_element_type=F32)
        parts = []
        for h in range(HEADS_PER_LANE_GROUP):
            g = jnp.where(past, gates[h * HEAD_DIM:(h + 1) * HEAD_DIM], -jnp.inf)
            taken = jnp.zeros(g.shape, jnp.bool_)
            for _ in range(MOBA_TOPK):
                gm = jnp.where(taken, -jnp.inf, g)
                m = jnp.max(gm, axis=0, keepdims=True)
                cand = jnp.logical_and(gm == m, jnp.logical_not(taken))
                idx = jnp.min(jnp.where(cand, blk_f, float(HEAD_DIM)), axis=0, keepdims=True)
                taken = jnp.logical_or(taken, blk_f == idx)
            chosen = jnp.logical_and(taken, past)
            parts.append(jnp.where(chosen, 0.0, MASK_BIAS).astype(BF16))
            parts.append(ext_ref[p * HEADS_PER_LANE_GROUP + h])
        out = lax.dot_general(eye_ref[...], jnp.concatenate(parts, axis=0), contract_last,
                              preferred_element_type=F32)
        width = HEADS_PER_LANE_GROUP * LANES
        sel_ref[0, :, p * width:(p + 1) * width] = out.astype(BF16)


def _gate(q, kbar, ext_t, eye):
    b, s, d = q.shape
    nb = s // MOBA_BLOCK
    return pl.pallas_call(
        _gate_kernel,
        out_shape=jax.ShapeDtypeStruct((b, s, N_HEADS * LANES), BF16),
        grid=(b, nb),
        in_specs=[
            pl.BlockSpec((1, MOBA_BLOCK, d), lambda i, j: (i, j, 0)),
            pl.BlockSpec((1, HEAD_DIM, d), lambda i, j: (i, 0, 0)),
            pl.BlockSpec((N_HEADS, HEAD_DIM, MOBA_BLOCK), lambda i, j: (0, 0, 0)),
            pl.BlockSpec((MOBA_BLOCK, MOBA_BLOCK), lambda i, j: (0, 0)),
        ],
        out_specs=pl.BlockSpec((1, MOBA_BLOCK, N_HEADS * LANES), lambda i, j: (i, j, 0)),
        compiler_params=_cparams(("parallel", "parallel")),
        name="moba_gate",
    )(q, kbar, ext_t, eye)


def _attn_kernel(first_chunk_ref, q_ref, sel_ref, k_ref, v_ref, kx_ref, o_ref, qa_scr, qd_scr, m_scr, acc_scr,
                 *, group, kv_blocks):
    step = pl.program_id(2)
    tq = group * MOBA_BLOCK
    tkv = kv_blocks * MOBA_BLOCK
    chunks = group // kv_blocks
    heads = range(HEADS_PER_LANE_GROUP)
    rows = HEADS_PER_LANE_GROUP * tq
    lane = lax.broadcasted_iota(jnp.int32, (tq, LANES), 1)
    contract_last = (((1,), (1,)), ((), ()))

    def visit(chunk, lhs_scr, first_query=0):
        start = pl.multiple_of(chunk * tkv, tkv)
        ka = jnp.concatenate([k_ref[0, pl.ds(start, tkv), :], kx_ref[pl.ds(start, tkv), :]], axis=1)
        vg = v_ref[0, pl.ds(start, tkv), :]
        kv_lane = lax.broadcasted_iota(jnp.int32, (tkv, LANES), 1)
        va = [jnp.where((kv_lane >= h * HEAD_DIM) & (kv_lane < (h + 1) * HEAD_DIM), vg, jnp.ones_like(vg))
              for h in heads]
        if first_query == 0:
            spans = [(0, rows)]
        else:
            spans = [(h * tq + first_query, (h + 1) * tq) for h in heads]
        for lo, hi in spans:
            s = lax.dot_general(lhs_scr[lo:hi], ka, contract_last, preferred_element_type=F32)
            if lhs_scr is qd_scr:
                query = lax.rem(lax.broadcasted_iota(jnp.int32, s.shape, 0) + lo, tq)
                s = jnp.where(query >= lax.broadcasted_iota(jnp.int32, s.shape, 1) + first_query, s, MASK_BIAS)
            m_prev = m_scr[lo:hi]
            m_new = jnp.maximum(m_prev, jnp.max(s, axis=-1, keepdims=True))
            alpha = jnp.exp2(m_prev - m_new)
            p = jnp.exp2((s - jnp.concatenate([m_new] * (tkv // LANES), axis=1)).astype(BF16))
            pv = [jnp.dot(p[max(lo, h * tq) - lo:min(hi, (h + 1) * tq) - lo], va[h], preferred_element_type=F32)
                  for h in heads if max(lo, h * tq) < min(hi, (h + 1) * tq)]
            acc_scr[lo:hi] = alpha * acc_scr[lo:hi] + jnp.concatenate(pv, axis=0)
            m_scr[lo:hi] = m_new

    qp = q_ref[0]
    row_block = lax.shift_right_logical(lax.broadcasted_iota(jnp.int32, (tq, LANES), 0),
                                        MOBA_BLOCK.bit_length() - 1) + step * group
    for h in heads:
        in_head = (lane >= h * HEAD_DIM) & (lane < (h + 1) * HEAD_DIM)
        qh = jnp.where(in_head, qp, jnp.zeros_like(qp))
        sx = sel_ref[0, :, h * LANES:(h + 1) * LANES]
        qa_scr[h * tq:(h + 1) * tq, :] = jnp.concatenate([qh, sx], axis=1)
        qd_scr[h * tq:(h + 1) * tq, :] = jnp.concatenate(
            [qh, jnp.where(lane == row_block, jnp.zeros_like(sx), sx)], axis=1)
    m_scr[...] = jnp.full((rows, LANES), MASK_BIAS, F32)
    acc_scr[...] = jnp.zeros((rows, LANES), F32)

    for u in range(chunks):
        visit(step * chunks + u, qd_scr, first_query=u * tkv)

    def body(it, carry):
        for u in range(chunks):
            visit(it * chunks + u, qa_scr)
        return carry

    first_chunk = first_chunk_ref[pl.program_id(1), step]
    first_it = lax.div(first_chunk + (chunks - 1), chunks)
    for u in range(chunks - 1):
        @pl.when(first_chunk + u < first_it * chunks)
        def _():
            visit(first_chunk + u, qa_scr)
    lax.fori_loop(first_it, step, body, 0)
    acc = acc_scr[...]
    out = acc / pltpu.roll(acc, HEAD_DIM, axis=1)
    o_ref[0] = jnp.where(lane < HEAD_DIM, out[:tq], out[tq:]).astype(BF16)


def _first_chunks(g_q_scaled, g_k, steps, tq, tkv):
    round_up = 1.0 + 2.0 ** -6
    qk = HEAD_DIM * jnp.max(jnp.abs(g_q_scaled)) * jnp.max(jnp.abs(g_k)) * round_up
    slopes = np.array([2.0 ** (-8.0 * (h + 1) / N_HEADS) for h in range(N_HEADS)]) * LOG2E * (1.0 - 2.0 ** -10)
    pair_slope = jnp.asarray(slopes.reshape(-1, HEADS_PER_LANE_GROUP).min(axis=1), F32)
    reach = (UNDERFLOW_LOG2 + 2.0 * qk) / pair_slope
    tile_start = jnp.arange(steps, dtype=F32) * tq
    first = jnp.floor((tile_start[None, :] - reach[:, None]) / tkv)
    first = jnp.where(jnp.isfinite(first), first, 0.0)
    return jnp.clip(first, 0.0, tile_start[None, :] / tkv).astype(jnp.int32)


def _attn(q, selx, k, v, kx, g_q_scaled, g_k, group=4, kv_blocks=2):
    b, s, d = q.shape
    pairs = d // LANES
    tq = group * MOBA_BLOCK
    first_chunk = _first_chunks(g_q_scaled, g_k, s // tq, tq, kv_blocks * MOBA_BLOCK)
    grid_spec = pltpu.PrefetchScalarGridSpec(
        num_scalar_prefetch=1,
        grid=(b, pairs, s // tq),
        in_specs=[
            pl.BlockSpec((1, tq, LANES), lambda bi, p, i, fi: (bi, i, p)),
            pl.BlockSpec((1, tq, HEADS_PER_LANE_GROUP * LANES), lambda bi, p, i, fi: (bi, i, p)),
            pl.BlockSpec((1, s, LANES), lambda bi, p, i, fi: (bi, 0, p)),
            pl.BlockSpec((1, s, LANES), lambda bi, p, i, fi: (bi, 0, p)),
            _resident((s, LANES), lambda bi, p, i, fi: (0, 0)),
        ],
        out_specs=pl.BlockSpec((1, tq, LANES), lambda bi, p, i, fi: (bi, i, p)),
        scratch_shapes=[
            pltpu.VMEM((HEADS_PER_LANE_GROUP * tq, 2 * LANES), BF16),
            pltpu.VMEM((HEADS_PER_LANE_GROUP * tq, 2 * LANES), BF16),
            pltpu.VMEM((HEADS_PER_LANE_GROUP * tq, LANES), F32),
            pltpu.VMEM((HEADS_PER_LANE_GROUP * tq, LANES), F32),
        ],
    )
    return pl.pallas_call(
        functools.partial(_attn_kernel, group=group, kv_blocks=kv_blocks),
        out_shape=jax.ShapeDtypeStruct((b, s, d), BF16),
        grid_spec=grid_spec,
        compiler_params=_cparams(("parallel", "parallel", "arbitrary")),
        name="moba_attn",
    )(first_chunk, q, selx, k, v, kx)


def _conv_kernel(x_ref, g_ref, mod_ref, w1_ref, b1_ref, wdw_ref, bdw_ref, lng_ref, lnb_ref,
                 w2_ref, b2_ref, o_ref, u_scr, c_scr, *, row_chunk):
    d = D_MODEL
    tm = x_ref.shape[1]
    x = x_ref[0]
    mod = mod_ref[0]

    @pl.when(pl.program_id(1) == 0)
    def _():
        u_scr[0:CONV_HALO, :] = jnp.zeros((CONV_HALO, d), F32)

    h = _modulated_norm(x, g_ref[...], mod).astype(BF16)
    u = jnp.dot(h, w1_ref[...], preferred_element_type=F32) + b1_ref[...]
    u_scr[CONV_HALO:CONV_HALO + tm, :] = u[:, :d] * jax.nn.sigmoid(u[:, d:])

    base = CONV_HALO - (CONV_WIDTH - 1)

    def chunk(r, carry):
        r0 = pl.multiple_of(r * row_chunk, row_chunk)
        for lc in range(d // LANES):
            ls = slice(lc * LANES, (lc + 1) * LANES)
            win = u_scr[pl.ds(r0, row_chunk + CONV_HALO), ls]
            acc = jnp.broadcast_to(bdw_ref[:, ls], (row_chunk, LANES))
            for shift in range(SUBLANES):
                taps = [w for w in range(CONV_WIDTH) if (base + w) % SUBLANES == shift]
                shifted = win if shift == 0 else pltpu.roll(win, win.shape[0] - shift, axis=0)
                for w in taps:
                    a0 = (base + w) // SUBLANES * SUBLANES
                    acc = acc + shifted[a0:a0 + row_chunk] * wdw_ref[w:w + 1, ls]
            c_scr[pl.ds(r0, row_chunk), ls] = acc
        return carry

    lax.fori_loop(0, tm // row_chunk, chunk, 0)
    u_scr[0:CONV_HALO, :] = u_scr[tm:tm + CONV_HALO, :]

    c = c_scr[...]
    mu = jnp.mean(c, axis=-1, keepdims=True)
    cc = c - mu
    var = jnp.mean(cc * cc, axis=-1, keepdims=True)
    z = _silu(cc * lax.rsqrt(var + EPS) * lng_ref[...] + lnb_ref[...]).astype(BF16)
    y = jnp.dot(z, w2_ref[...], preferred_element_type=F32) + b2_ref[...]
    o_ref[0] = x + mod[2:3, :] * y


def _conv(x, g, mod, w1, b1, wdw, bdw, lng, lnb, w2, b2, tm=512, row_chunk=64):
    b, s, d = x.shape
    tok = lambda i, j: (i, j, 0)
    vec = lambda n: pl.BlockSpec((1, n), lambda i, j: (0, 0))
    return pl.pallas_call(
        functools.partial(_conv_kernel, row_chunk=row_chunk),
        out_shape=jax.ShapeDtypeStruct(x.shape, F32),
        grid=(b, s // tm),
        in_specs=[
            pl.BlockSpec((1, tm, d), tok),
            vec(d),
            pl.BlockSpec((1, 3, d), lambda i, j: (i, 0, 0)),
            _resident((d, 2 * d), lambda i, j: (0, 0)),
            vec(2 * d),
            pl.BlockSpec((CONV_WIDTH, d), lambda i, j: (0, 0)),
            vec(d), vec(d), vec(d),
            _resident((d, d), lambda i, j: (0, 0)),
            vec(d),
        ],
        out_specs=pl.BlockSpec((1, tm, d), tok),
        scratch_shapes=[pltpu.VMEM((CONV_HALO + tm, d), F32), pltpu.VMEM((tm, d), F32)],
        compiler_params=_cparams(("parallel", "arbitrary")),
        name="conformer_conv",
    )(x, g.reshape(1, d), mod, w1.astype(BF16), b1.reshape(1, 2 * d), wdw, bdw.reshape(1, d),
      lng.reshape(1, d), lnb.reshape(1, d), w2.astype(BF16), b2.reshape(1, d))


def _bf16_pieces(x, n=3):
    out = []
    rest = np.float64(x)
    for _ in range(n):
        p = np.float64(np.asarray(rest, np.float32).astype(jnp.bfloat16).astype(np.float32))
        out.append(float(p))
        rest = rest - p
    return out


def _alibi_query_rows():
    ext = np.zeros((N_HEADS, HEAD_DIM, MOBA_BLOCK), np.float32)
    for h in range(N_HEADS):
        slope = np.float32(2.0 ** (-8.0 * (h + 1) / N_HEADS))
        pieces = _bf16_pieces(float(slope) * LOG2E)
        for e, piece in enumerate(pieces + [MOBA_BLOCK * p for p in pieces]):
            ext[h, e, :] = piece
    return jnp.asarray(ext, dtype=BF16)


def _key_side_table(s):
    pos = np.arange(s)
    blk, off = pos // MOBA_BLOCK, pos % MOBA_BLOCK
    kx = np.zeros((s, LANES), np.float32)
    kx[pos, blk] = 1.0
    kx[:, HEAD_DIM:HEAD_DIM + 3] = off[:, None]
    kx[:, HEAD_DIM + 3:HEAD_DIM + 6] = blk[:, None]
    return jnp.asarray(kx, dtype=BF16)


def _head_indicator():
    e = np.zeros((D_MODEL, LANES), np.float32)
    e[np.arange(D_MODEL), np.arange(D_MODEL) // HEAD_DIM] = 1.0
    et = np.concatenate([e.T, e.T], axis=0)
    return jnp.asarray(e, dtype=BF16), jnp.asarray(et, dtype=BF16)


def _moba_mixer(x, g, mod, w_qkv, g_q, g_k):
    s = x.shape[1]
    assert s % MOBA_BLOCK == 0 and s // MOBA_BLOCK <= HEAD_DIM
    e_mat, et_mat = _head_indicator()
    q_scale = LOG2E / math.sqrt(HEAD_DIM)
    gq_row = (jnp.tile(g_q, N_HEADS) * q_scale).reshape(1, D_MODEL)
    gk_row = jnp.tile(g_k, N_HEADS).reshape(1, D_MODEL)
    q, k, v, kbar = _qkv(x, g, mod, w_qkv, gq_row, gk_row, e_mat, et_mat)
    nb = s // MOBA_BLOCK
    kbar = jnp.pad(kbar.reshape(-1, nb, D_MODEL), ((0, 0), (0, HEAD_DIM - nb), (0, 0)))
    selx = _gate(q, kbar, _alibi_query_rows(), jnp.eye(MOBA_BLOCK, dtype=BF16))
    return _attn(q, selx, k, v, _key_side_table(s), g_q * q_scale, g_k)


def kernel(x, c, norm_g, ada_w, ada_b, ffn_w_in, ffn_w_out, attn_w_qkv, attn_g_q, attn_g_k, attn_w_o,
           conv_w_pw1, conv_b_pw1, conv_w_dw, conv_b_dw, conv_ln_g, conv_ln_b, conv_w_pw2, conv_b_pw2):
    depth = norm_g.shape[0]
    b = x.shape[0]
    mods = _ada(c, ada_w, ada_b).reshape(depth, b, N_SUB, 3, D_MODEL)
    w_in_all, w_out_all = ffn_w_in.astype(BF16), ffn_w_out.astype(BF16)
    for i in range(depth):
        mod = lambda j: mods[i, :, j]
        x = _ffn(x, norm_g[i, 0], mod(0), w_in_all, w_out_all, (i, 0))
        m = i // 2
        mixer = None
        if i % 2 == 0:
            o = _moba_mixer(x, norm_g[i, 1], mod(1), attn_w_qkv[m], attn_g_q[m], attn_g_k[m])
            mixer = (o, attn_w_o[m], mod(1))
        else:
            x = _conv(x, norm_g[i, 1], mod(1), conv_w_pw1[m], conv_b_pw1[m], conv_w_dw[m], conv_b_dw[m],
                      conv_ln_g[m], conv_ln_b[m], conv_w_pw2[m], conv_b_pw2[m])
        x = _ffn(x, norm_g[i, 2], mod(2), w_in_all, w_out_all, (i, 1), mixer=mixer)
    return x
```
